```python
import math
import jax, jax.numpy as jnp
from jax import lax
import numpy as np

D_MODEL = 2048
BATCH = 2
SEQ = 16384
DEPTH = 1

MEM_LEN = 256
RWKV_HEADS = 16
RWKV_HEAD_DIM = 64
RWKV_WIDTH = RWKV_HEADS * RWKV_HEAD_DIM
DECAY_LORA = 96
ICLR_LORA = 96
GATE_LORA = 256
FOX_HEADS = 8
FOX_HEAD_DIM = 64
FOX_WIDTH = FOX_HEADS * FOX_HEAD_DIM
MEM_HEADS = 4
MEM_HEAD_DIM = 128
MEM_WIDTH = MEM_HEADS * MEM_HEAD_DIM
MIX_WIDTH = RWKV_WIDTH + FOX_WIDTH + MEM_WIDTH
Q_BLOCK = 128
RWKV_PROJ_WIDTH = 3 * RWKV_WIDTH + DECAY_LORA + ICLR_LORA + GATE_LORA
FOX_PROJ_WIDTH = 3 * FOX_WIDTH + FOX_HEADS
IN_WIDTH = RWKV_PROJ_WIDTH + FOX_PROJ_WIDTH + MEM_WIDTH
N_EXPERTS = 32
TOP_K = 4
D_EXPERT = D_MODEL
SWIGLU_LIMIT = 7.0
SWIGLU_ALPHA = 1.702
MOE_BLOCK = 128
LN_EPS = 1e-5
RWKV_GN_EPS = 64e-5
DEEPNORM_ALPHA = (2 * DEPTH) ** 0.25
DEEPNORM_BETA = (8 * DEPTH) ** -0.25

kernel_name = "hybrid_rwkv7_fox_memxattn_moe_deepnorm"


def layer_norm(x, g, b, eps=LN_EPS):
    xf = x.astype(jnp.float32)
    mean = xf.mean(-1, keepdims=True)
    var = jnp.square(xf - mean).mean(-1, keepdims=True)
    y = (xf - mean) * lax.rsqrt(var + eps) * g.astype(jnp.float32) + b.astype(jnp.float32)
    return y.astype(x.dtype)


def token_shift_lerp(p, mu):
    prev = jnp.pad(p[:, :-1], ((0, 0), (1, 0), (0, 0)))
    return p + mu * (prev - p)


def rwkv7_scan(r, decay, k, v, a_vec, b_vec):
    Bq, T, H, N = r.shape
    xs = (jnp.moveaxis(r, 1, 0), jnp.moveaxis(decay, 1, 0), jnp.moveaxis(k, 1, 0),
          jnp.moveaxis(v, 1, 0), jnp.moveaxis(a_vec, 1, 0), jnp.moveaxis(b_vec, 1, 0))
    s0 = jnp.zeros((Bq, H, N, N), jnp.float32)

    def step(S, inp):
        r_t, w_t, k_t, v_t, a_t, b_t = inp
        sa = jnp.einsum('bhij,bhj->bhi', S, a_t)
        S = S * w_t[:, :, None, :] + sa[..., None] * b_t[:, :, None, :] + v_t[..., None] * k_t[:, :, None, :]
        return S, jnp.einsum('bhij,bhj->bhi', S, r_t)

    _, ys = lax.scan(step, s0, xs)
    return jnp.moveaxis(ys, 0, 1)


def rwkv7_group(p, mu, w0, w_decay_up, a0, w_iclr_up, w_gate_up, k_k, k_a, r_k, gn_g, gn_b):
    Bq, T, _ = p.shape
    f32 = jnp.float32
    ps = token_shift_lerp(p, mu).astype(f32)
    c0 = 3 * RWKV_WIDTH
    r = ps[..., :RWKV_WIDTH]
    k = ps[..., RWKV_WIDTH:2 * RWKV_WIDTH]
    v = ps[..., 2 * RWKV_WIDTH:c0]
    dw = ps[..., c0:c0 + DECAY_LORA]
    da = ps[..., c0 + DECAY_LORA:c0 + DECAY_LORA + ICLR_LORA]
    dg = ps[..., c0 + DECAY_LORA + ICLR_LORA:]
    w_log = -jax.nn.softplus(-(w0.astype(f32) + jnp.tanh(dw) @ w_decay_up.astype(f32))) - 0.5
    decay = jnp.exp(-jnp.exp(w_log))
    a = jax.nn.sigmoid(a0.astype(f32) + da @ w_iclr_up.astype(f32))
    g = jax.nn.sigmoid(dg) @ w_gate_up.astype(f32)
    shp = (Bq, T, RWKV_HEADS, RWKV_HEAD_DIM)
    kk = (k * k_k.astype(f32)).reshape(shp)
    kk = kk / jnp.maximum(jnp.linalg.norm(kk, axis=-1, keepdims=True), 1e-12)
    k = k * (1.0 + (a - 1.0) * k_a.astype(f32))
    rh, kh, vh, ah = r.reshape(shp), k.reshape(shp), v.reshape(shp), a.reshape(shp)
    y = rwkv7_scan(rh, decay.reshape(shp), kh, vh, -kk, kk * ah)
    mean = y.mean(-1, keepdims=True)
    var = jnp.square(y - mean).mean(-1, keepdims=True)
    y = ((y - mean) * lax.rsqrt(var + RWKV_GN_EPS)).reshape(Bq, T, RWKV_WIDTH)
    y = y * gn_g.astype(f32) + gn_b.astype(f32)
    bonus = (rh * kh * r_k.astype(f32)).sum(-1, keepdims=True) * vh
    y = y + bonus.reshape(Bq, T, RWKV_WIDTH)
    return y * g


def fox_attention(q, k, v, c):
    Bq, H, T, Dh = q.shape
    nb = T // Q_BLOCK
    scale = Dh ** -0.5

    def blocks(t):
        return jnp.moveaxis(t.reshape((Bq, H, nb, Q_BLOCK) + t.shape[3:]), 2, 0)

    qb, kb, vb, cb = blocks(q), blocks(k), blocks(v), blocks(c)
    offs = jnp.arange(Q_BLOCK)

    def one_query_block(args):
        i, qi, ci = args
        qpos = i * Q_BLOCK + offs

        def body(j, carry):
            m, l, acc = carry
            s = jnp.einsum('bhqd,bhkd->bhqk', qi, kb[j]) * scale + ci[..., :, None] - cb[j][..., None, :]
            causal = (j * Q_BLOCK + offs)[None, :] <= qpos[:, None]
            s = jnp.where(causal, s, -jnp.inf)
            m_new = jnp.maximum(m, s.max(-1))
            pr = jnp.exp(s - m_new[..., None])
            corr = jnp.exp(m - m_new)
            acc = acc * corr[..., None] + jnp.einsum('bhqk,bhkd->bhqd', pr, vb[j])
            return (m_new, l * corr + pr.sum(-1), acc)

        init = (jnp.full((Bq, H, Q_BLOCK), -jnp.inf, jnp.float32),
                jnp.zeros((Bq, H, Q_BLOCK), jnp.float32),
                jnp.zeros((Bq, H, Q_BLOCK, Dh), jnp.float32))
        _, l, acc = lax.fori_loop(0, i + 1, body, init)
        return acc / l[..., None]

    out = lax.map(one_query_block, (jnp.arange(nb), qb, cb))
    return jnp.moveaxis(out, 0, 2).reshape(Bq, H, T, Dh)


def fox_group(p, b_f):
    Bq, T, _ = p.shape
    f32 = jnp.float32
    pf = p.astype(f32)

    def heads(t):
        return t.reshape(Bq, T, FOX_HEADS, FOX_HEAD_DIM).transpose(0, 2, 1, 3)

    q = heads(pf[..., :FOX_WIDTH])
    k = heads(pf[..., FOX_WIDTH:2 * FOX_WIDTH])
    v = heads(pf[..., 2 * FOX_WIDTH:3 * FOX_WIDTH])
    log_f = jax.nn.log_sigmoid(pf[..., 3 * FOX_WIDTH:] + b_f.astype(f32))
    c = lax.cumsum(log_f, axis=1).transpose(0, 2, 1)
    o = fox_attention(q, k, v, c)
    return o.transpose(0, 2, 1, 3).reshape(Bq, T, FOX_WIDTH)


def memory_group(q_cols, mem_n, w_mem_kv):
    Bq, T, _ = q_cols.shape
    f32 = jnp.float32
    q = q_cols.astype(f32).reshape(Bq, T, MEM_HEADS, MEM_HEAD_DIM).transpose(0, 2, 1, 3)
    kv = (mem_n @ w_mem_kv).astype(f32)
    km = kv[..., :MEM_WIDTH].reshape(Bq, -1, MEM_HEADS, MEM_HEAD_DIM).transpose(0, 2, 1, 3)
    vm = kv[..., MEM_WIDTH:].reshape(Bq, -1, MEM_HEADS, MEM_HEAD_DIM).transpose(0, 2, 1, 3)
    s = jnp.einsum('bhtd,bhmd->bhtm', q, km) * (MEM_HEAD_DIM ** -0.5)
    o = jnp.einsum('bhtm,bhmd->bhtd', jax.nn.softmax(s, axis=-1), vm)
    return o.transpose(0, 2, 1, 3).reshape(Bq, T, MEM_WIDTH)


def moe_ffn(h, router_w, router_b, w1, b1, w2, b2):
    Bq, T, D = h.shape
    n = Bq * T
    hf = h.reshape(n, D)
    logits = (hf @ router_w + router_b).astype(jnp.float32)
    top_vals, top_idx = lax.top_k(logits, TOP_K)
    gates = jax.nn.softmax(top_vals, axis=-1)
    flat_e = top_idx.reshape(-1)
    flat_tok = jnp.repeat(jnp.arange(n, dtype=jnp.int32), TOP_K)
    flat_g = gates.reshape(-1)
    n_assign = n * TOP_K
    order = jnp.argsort(flat_e)
    sorted_e = flat_e[order]
    counts = jnp.bincount(flat_e, length=N_EXPERTS)
    padded = ((counts + MOE_BLOCK - 1) // MOE_BLOCK) * MOE_BLOCK
    start = jnp.cumsum(counts) - counts
    pend = jnp.cumsum(padded)
    pstart = pend - padded
    dest = pstart[sorted_e] + (jnp.arange(n_assign) - start[sorted_e])
    cap = n_assign + N_EXPERTS * MOE_BLOCK
    n_blocks = cap // MOE_BLOCK
    buf_tok = jnp.zeros((cap,), jnp.int32).at[dest].set(flat_tok[order])
    buf_gate = jnp.zeros((cap,), jnp.float32).at[dest].set(flat_g[order])
    block_e = jnp.clip(jnp.searchsorted(pend, jnp.arange(n_blocks) * MOE_BLOCK, side='right'), 0, N_EXPERTS - 1)

    def expert_block(args):
        tok, e, gate = args
        hb = hf[tok] @ w1[e] + b1[e]
        glu = jnp.minimum(hb[:, :D_EXPERT], SWIGLU_LIMIT)
        lin = jnp.clip(hb[:, D_EXPERT:], -SWIGLU_LIMIT, SWIGLU_LIMIT)
        act = (lin + 1.0) * glu * jax.nn.sigmoid(SWIGLU_ALPHA * glu)
        return (act @ w2[e] + b2[e]) * gate[:, None].astype(hf.dtype)

    yb = lax.map(expert_block, (buf_tok.reshape(n_blocks, MOE_BLOCK), block_e,
                                buf_gate.reshape(n_blocks, MOE_BLOCK)))
    out = jax.ops.segment_sum(yb.reshape(cap, D), buf_tok, num_segments=n)
    return out.reshape(Bq, T, D).astype(h.dtype)


def setup_inputs(seed: int = 0) -> dict:
    key = jax.random.key(seed)
    ks = jax.random.split(key, 32)
    f32 = jnp.float32
    L = DEPTH

    def nrm(k, shape, scale):
        return jax.random.normal(k, shape, f32) * scale

    return {
        "x": nrm(ks[0], (BATCH, SEQ, D_MODEL), 1.0),
        "mem": nrm(ks[1], (BATCH, MEM_LEN, D_MODEL), 1.0),
        "emb_ln_g": 1.0 + nrm(ks[2], (D_MODEL,), 0.02),
        "emb_ln_b": nrm(ks[3], (D_MODEL,), 0.01),
        "w_in": nrm(ks[4], (L, D_MODEL, IN_WIDTH), D_MODEL ** -0.5),
        "rwkv_mu": jax.random.uniform(ks[5], (L, RWKV_PROJ_WIDTH), f32),
        "rwkv_w0": jax.random.uniform(ks[6], (L, RWKV_WIDTH), f32, -6.0, -1.0),
        "rwkv_w_decay_up": nrm(ks[7], (L, DECAY_LORA, RWKV_WIDTH), DECAY_LORA ** -0.5),
        "rwkv_a0": nrm(ks[8], (L, RWKV_WIDTH), 0.5),
        "rwkv_w_iclr_up": nrm(ks[9], (L, ICLR_LORA, RWKV_WIDTH), ICLR_LORA ** -0.5),
        "rwkv_w_gate_up": nrm(ks[10], (L, GATE_LORA, RWKV_WIDTH), GATE_LORA ** -0.5),
        "rwkv_k_k": 0.85 + nrm(ks[11], (L, RWKV_WIDTH), 0.05),
        "rwkv_k_a": 1.0 + nrm(ks[12], (L, RWKV_WIDTH), 0.05),
        "rwkv_r_k": nrm(ks[13], (L, RWKV_HEADS, RWKV_HEAD_DIM), 0.1),
        "rwkv_gn_g": 1.0 + nrm(ks[14], (L, RWKV_WIDTH), 0.02),
        "rwkv_gn_b": nrm(ks[15], (L, RWKV_WIDTH), 0.01),
        "fox_b_f": jax.random.uniform(ks[16], (L, FOX_HEADS), f32, 1.0, 4.0),
        "mem_ln_g": 1.0 + nrm(ks[17], (D_MODEL,), 0.02),
        "mem_ln_b": nrm(ks[18], (D_MODEL,), 0.01),
        "w_mem_kv": nrm(ks[19], (L, D_MODEL, 2 * MEM_WIDTH), D_MODEL ** -0.5),
        "w_o": nrm(ks[20], (L, MIX_WIDTH, D_MODEL), MIX_WIDTH ** -0.5 * DEEPNORM_BETA),
        "ln1_g": 1.0 + nrm(ks[21], (L, D_MODEL), 0.02),
        "ln1_b": nrm(ks[22], (L, D_MODEL), 0.01),
        "router_w": nrm(ks[23], (L, D_MODEL, N_EXPERTS), D_MODEL ** -0.5),
        "router_b": nrm(ks[24], (L, N_EXPERTS), 0.01),
        "expert_w1": nrm(ks[25], (L, N_EXPERTS, D_MODEL, 2 * D_EXPERT), D_MODEL ** -0.5),
        "expert_b1": nrm(ks[26], (L, N_EXPERTS, 2 * D_EXPERT), 0.01),
        "expert_w2": nrm(ks[27], (L, N_EXPERTS, D_EXPERT, D_MODEL), D_EXPERT ** -0.5 * DEEPNORM_BETA),
        "expert_b2": nrm(ks[28], (L, N_EXPERTS, D_MODEL), 0.01),
        "ln2_g": 1.0 + nrm(ks[29], (L, D_MODEL), 0.02),
        "ln2_b": nrm(ks[30], (L, D_MODEL), 0.01),
    }


def reference(x, mem, emb_ln_g, emb_ln_b, w_in, rwkv_mu, rwkv_w0, rwkv_w_decay_up, rwkv_a0,
              rwkv_w_iclr_up, rwkv_w_gate_up, rwkv_k_k, rwkv_k_a, rwkv_r_k, rwkv_gn_g, rwkv_gn_b,
              fox_b_f, mem_ln_g, mem_ln_b, w_mem_kv, w_o, ln1_g, ln1_b, router_w, router_b,
              expert_w1, expert_b1, expert_w2, expert_b2, ln2_g, ln2_b):
    h = layer_norm(x, emb_ln_g, emb_ln_b)
    mem_n = layer_norm(mem, mem_ln_g, mem_ln_b)
    for l in range(DEPTH):
        p = h @ w_in[l]
        p_rwkv = p[..., :RWKV_PROJ_WIDTH]
        p_fox = p[..., RWKV_PROJ_WIDTH:RWKV_PROJ_WIDTH + FOX_PROJ_WIDTH]
        p_memq = p[..., RWKV_PROJ_WIDTH + FOX_PROJ_WIDTH:]
        y_rwkv = rwkv7_group(p_rwkv, rwkv_mu[l], rwkv_w0[l], rwkv_w_decay_up[l], rwkv_a0[l],
                             rwkv_w_iclr_up[l], rwkv_w_gate_up[l], rwkv_k_k[l], rwkv_k_a[l],
                             rwkv_r_k[l], rwkv_gn_g[l], rwkv_gn_b[l])
        y_fox = fox_group(p_fox, fox_b_f[l])
        y_mem = memory_group(p_memq, mem_n, w_mem_kv[l])
        mixed = jnp.concatenate([y_rwkv, y_fox, y_mem], axis=-1).astype(h.dtype) @ w_o[l]
        h = layer_norm(DEEPNORM_ALPHA * h + mixed, ln1_g[l], ln1_b[l])
        ffn = moe_ffn(h, router_w[l], router_b[l], expert_w1[l], expert_b1[l], expert_w2[l], expert_b2[l])
        h = layer_norm(DEEPNORM_ALPHA * h + ffn, ln2_g[l], ln2_b[l])
    return h.astype(x.dtype)
```

```python
import functools
import math

import jax
import jax.numpy as jnp
from jax import lax
from jax.experimental import pallas as pl
from jax.experimental.pallas import tpu as pltpu

F32 = jnp.float32
BF16 = jnp.bfloat16
HIGHEST = lax.Precision.HIGHEST

LANES = 128
HEAD64 = 64
VMEM_LIMIT = 56 * 1024 * 1024

LN_EPS = 1e-5
RWKV_GN_EPS = 64e-5
SWIGLU_LIMIT = 7.0
SWIGLU_ALPHA = 1.702
TOP_K = 4


def _cparams(sem):
    return pltpu.CompilerParams(dimension_semantics=sem, vmem_limit_bytes=VMEM_LIMIT)


def _dot(a, b, precision=None):
    return jnp.dot(a, b, preferred_element_type=F32, precision=precision)


def _dot_nt(a, b, precision=None):
    return lax.dot_general(a, b, (((1,), (1,)), ((), ())), preferred_element_type=F32, precision=precision)


def _dot_tn(a, b, precision=None):
    return lax.dot_general(a, b, (((0,), (0,)), ((), ())), preferred_element_type=F32, precision=precision)


def _layer_norm_rows(x, g, b, eps):
    mean = jnp.mean(x, axis=-1, keepdims=True)
    xc = x - mean
    var = jnp.mean(xc * xc, axis=-1, keepdims=True)
    return xc * lax.rsqrt(var + eps) * g + b


def _sigmoid(x):
    return 1.0 / (1.0 + jnp.exp(-x))


def _ln_matmul_kernel(x_ref, g_ref, b_ref, w_ref, o_ref, xn_ref):
    @pl.when(pl.program_id(1) == 0)
    def _():
        xn_ref[...] = _layer_norm_rows(x_ref[...], g_ref[...], b_ref[...], LN_EPS).astype(BF16)

    o_ref[...] = _dot(xn_ref[...], w_ref[...]).astype(o_ref.dtype)


def ln_matmul(x, g, b, w, out_dtype, bm, bn):
    n, d = x.shape
    m = w.shape[1]
    assert n % bm == 0 and m % bn == 0
    return pl.pallas_call(
        _ln_matmul_kernel,
        grid=(n // bm, m // bn),
        in_specs=[
            pl.BlockSpec((bm, d), lambda i, j: (i, 0)),
            pl.BlockSpec((1, d), lambda i, j: (0, 0)),
            pl.BlockSpec((1, d), lambda i, j: (0, 0)),
            pl.BlockSpec((d, bn), lambda i, j: (0, j)),
        ],
        out_specs=pl.BlockSpec((bm, bn), lambda i, j: (i, j)),
        out_shape=jax.ShapeDtypeStruct((n, m), out_dtype),
        scratch_shapes=[pltpu.VMEM((bm, d), BF16)],
        compiler_params=_cparams(("parallel", "arbitrary")),
        name="ln_matmul",
    )(x, g.reshape(1, d), b.reshape(1, d), w)


RWKV_CHUNK = 64


def _rwkv_kernel(r_ref, k_ref, v_ref, dw_ref, da_ref, dg_ref,
                 mur_ref, muk_ref, muv_ref, mudw_ref, muda_ref, mudg_ref,
                 w0_ref, a0_ref, kk_ref, ka_ref, rk_ref, gng_ref, gnb_ref,
                 wd_ref, wa_ref, wg_ref,
                 o_ref,
                 cr_ref, ck_ref, cv_ref, cdw_ref, cda_ref, cdg_ref, h_ref,
                 sr_ref, slw_ref, sk_ref, sv_ref, sa_ref, sb_ref, sy_ref):
    tt = r_ref.shape[0]
    C = RWKV_CHUNK
    C2 = 2 * C

    @pl.when(pl.program_id(2) == 0)
    def _():
        for c in (cr_ref, ck_ref, cv_ref, cdw_ref, cda_ref, cdg_ref, h_ref):
            c[...] = jnp.zeros_like(c)

    def shift(x_ref, mu_ref, carry_ref):
        x = x_ref[...]
        rows = lax.broadcasted_iota(jnp.int32, x.shape, 0)
        prev = jnp.where(rows == 0, carry_ref[0:1, :], pltpu.roll(x, 1, axis=0))
        carry_ref[0:1, :] = x[tt - 1:tt, :]
        return x + mu_ref[...] * (prev - x)

    r = shift(r_ref, mur_ref, cr_ref)
    k = shift(k_ref, muk_ref, ck_ref)
    v = shift(v_ref, muv_ref, cv_ref)
    dw = shift(dw_ref, mudw_ref, cdw_ref)
    da = shift(da_ref, muda_ref, cda_ref)
    dg = shift(dg_ref, mudg_ref, cdg_ref)

    lane = lax.broadcasted_iota(jnp.int32, (LANES, LANES), 1)
    row = lax.broadcasted_iota(jnp.int32, (LANES, LANES), 0)
    seg_ones = jnp.where((lane // HEAD64) == (row // HEAD64), 1.0, 0.0).astype(F32)

    def segsum(x):
        return _dot(x, seg_ones, HIGHEST)

    z = w0_ref[...] + _dot(jnp.tanh(dw), wd_ref[...], HIGHEST)
    softplus_neg = jnp.maximum(-z, 0.0) + jnp.log(1.0 + jnp.exp(-jnp.abs(z)))
    lw = -jnp.exp(-softplus_neg - 0.5)
    a = _sigmoid(a0_ref[...] + _dot(da, wa_ref[...], HIGHEST))
    gate = _dot(_sigmoid(dg), wg_ref[...], HIGHEST)
    kk = k * kk_ref[...]
    kk = kk / jnp.maximum(jnp.sqrt(segsum(kk * kk)), 1e-12)
    k2 = k * (1.0 + (a - 1.0) * ka_ref[...])
    bonus = segsum(r * k2 * rk_ref[...]) * v

    sr_ref[...] = r
    slw_ref[...] = lw
    sk_ref[...] = k2
    sv_ref[...] = v
    sa_ref[...] = -kk
    sb_ref[...] = kk * a

    lane_c = lax.broadcasted_iota(jnp.int32, (C, LANES), 1)
    head0 = lane_c < HEAD64
    rr = lax.broadcasted_iota(jnp.int32, (C2, C2), 0)
    cc = lax.broadcasted_iota(jnp.int32, (C2, C2), 1)
    same_head = (rr // C) == (cc // C)
    strict = same_head & ((rr % C) > (cc % C))
    incl = same_head & ((rr % C) >= (cc % C))
    eye2 = jnp.where(rr == cc, 1.0, 0.0).astype(F32)
    rc = lax.broadcasted_iota(jnp.int32, (C, C), 0)
    cc1 = lax.broadcasted_iota(jnp.int32, (C, C), 1)
    tri_incl = jnp.where(rc >= cc1, 1.0, 0.0).astype(F32)
    diag_mask = row == lane

    def stack(x):
        return jnp.concatenate([jnp.where(head0, x, 0.0), jnp.where(head0, 0.0, x)], axis=0)

    def chunk(c, carry):
        sl = pl.ds(pl.multiple_of(c * C, C), C)
        rc_, lwc, kc, vc, ac, bc = sr_ref[sl, :], slw_ref[sl, :], sk_ref[sl, :], sv_ref[sl, :], sa_ref[sl, :], sb_ref[sl, :]
        lg = _dot(tri_incl, lwc, HIGHEST)
        lg_last = lg[C - 1:C, :]
        e_pos = jnp.exp(lg)
        e_neg = jnp.exp(-lg)
        e_end = jnp.exp(lg_last - lg)
        at = stack(ac * jnp.exp(lg - lwc))
        rt = stack(rc_ * e_pos)
        bt = stack(bc * e_neg)
        kt = stack(kc * e_neg)
        bh = stack(bc * e_end)
        kh = stack(kc * e_end)
        vs = stack(vc)

        a_ab = jnp.where(strict, _dot_nt(at, bt, HIGHEST), 0.0)
        a_ak = jnp.where(strict, _dot_nt(at, kt, HIGHEST), 0.0)
        a_rb = jnp.where(incl, _dot_nt(rt, bt, HIGHEST), 0.0)
        a_rk = jnp.where(incl, _dot_nt(rt, kt, HIGHEST), 0.0)

        inv = eye2 + a_ab
        pw = a_ab
        for _ in range(int(math.log2(C)) - 1):
            pw = _dot(pw, pw, HIGHEST)
            inv = inv + _dot(inv, pw, HIGHEST)

        atp = _dot(inv, at, HIGHEST)
        u0 = _dot(inv, _dot(a_ak, vs, HIGHEST), HIGHEST)
        rp = rt + _dot(a_rb, atp, HIGHEST)
        y0 = _dot(a_rb, u0, HIGHEST) + _dot(a_rk, vs, HIGHEST)

        h0 = h_ref[...]
        y = _dot(rp, h0, HIGHEST) + y0
        sy_ref[sl, :] = y[0:C, :] + y[C:C2, :]
        u = u0 + _dot(atp, h0, HIGHEST)
        dg_mat = jnp.where(diag_mask, jnp.exp(lg_last), 0.0)
        h_ref[...] = _dot(dg_mat, h0, HIGHEST) + _dot_tn(bh, u, HIGHEST) + _dot_tn(kh, vs, HIGHEST)
        return carry

    lax.fori_loop(0, tt // C, chunk, 0)

    y = sy_ref[...]
    mean = segsum(y) * (1.0 / HEAD64)
    yc = y - mean
    var = segsum(yc * yc) * (1.0 / HEAD64)
    yn = yc * lax.rsqrt(var + RWKV_GN_EPS) * gng_ref[...] + gnb_ref[...]
    o_ref[...] = ((yn + bonus) * gate).astype(o_ref.dtype)


def rwkv_mix(p, mu, w0, a0, k_k, k_a, r_k, gn_g, gn_b, wd, wa, wg, *, width, tt, out_dtype):
    bsz, t_len, _ = p.shape
    assert width % LANES == 0 and (3 * width) % 256 == 0 and t_len % tt == 0 and tt % RWKV_CHUNK == 0
    npair = width // LANES
    nw = width // LANES
    col_dw = 3 * nw
    col_da = 3 * nw + 1
    col_dg = (3 * width + 256) // 256

    def tok(width_, col):
        return pl.BlockSpec((None, tt, width_), lambda b, h, t: (b, t, col(h)))

    def par(rows, width_, col):
        return pl.BlockSpec((rows, width_), lambda b, h, t: (0, col(h)))

    vec = lambda a_: a_.reshape(1, -1)
    cols = [lambda h: h, lambda h: nw + h, lambda h: 2 * nw + h,
            lambda h: col_dw, lambda h: col_da, lambda h: col_dg]
    widths = [LANES, LANES, LANES, LANES, LANES, 256]
    in_specs = [tok(w_, c_) for w_, c_ in zip(widths, cols)]
    in_specs += [par(1, w_, c_) for w_, c_ in zip(widths, cols)]
    in_specs += [par(1, LANES, lambda h: h)] * 7
    in_specs += [par(LANES, LANES, lambda h: h), par(LANES, LANES, lambda h: h), par(256, LANES, lambda h: h)]
    mu2 = vec(mu)
    args = [p] * 6 + [mu2] * 6 + [vec(w0), vec(a0), vec(k_k), vec(k_a), vec(r_k), vec(gn_g), vec(gn_b), wd, wa, wg]
    carry = lambda w_: pltpu.VMEM((8, w_), F32)
    tile = pltpu.VMEM((tt, LANES), F32)
    return pl.pallas_call(
        _rwkv_kernel,
        grid=(bsz, npair, t_len // tt),
        in_specs=in_specs,
        out_specs=pl.BlockSpec((None, tt, LANES), lambda b, h, t: (b, t, h)),
        out_shape=jax.ShapeDtypeStruct((bsz, t_len, width), out_dtype),
        scratch_shapes=[carry(LANES)] * 5 + [carry(256), pltpu.VMEM((LANES, LANES), F32)] + [tile] * 7,
        compiler_params=_cparams(("parallel", "parallel", "arbitrary")),
        name="rwkv_mix",
    )(*args)


def _fox_cumsum_kernel(f_ref, bf_ref, o_ref, carry_ref):
    @pl.when(pl.program_id(1) == 0)
    def _():
        carry_ref[...] = jnp.zeros_like(carry_ref)

    tt = f_ref.shape[0]
    z = f_ref[...] + bf_ref[...]
    log_f = -(jnp.maximum(-z, 0.0) + jnp.log(1.0 + jnp.exp(-jnp.abs(z))))
    rr = lax.broadcasted_iota(jnp.int32, (tt, tt), 0)
    cc = lax.broadcasted_iota(jnp.int32, (tt, tt), 1)
    tri = jnp.where(rr >= cc, 1.0, 0.0).astype(F32)
    c = _dot(tri, log_f, HIGHEST) + carry_ref[0:1, :]
    o_ref[...] = c
    carry_ref[0:1, :] = c[tt - 1:tt, :]


def fox_cumsum(f_logits, b_f, tt):
    bsz, t_len, _ = f_logits.shape
    return pl.pallas_call(
        _fox_cumsum_kernel,
        grid=(bsz, t_len // tt),
        in_specs=[pl.BlockSpec((None, tt, LANES), lambda b, t: (b, t, 0)),
                  pl.BlockSpec((1, LANES), lambda b, t: (0, 0))],
        out_specs=pl.BlockSpec((None, tt, LANES), lambda b, t: (b, t, 0)),
        out_shape=jax.ShapeDtypeStruct((bsz, t_len, LANES), F32),
        scratch_shapes=[pltpu.VMEM((8, LANES), F32)],
        compiler_params=_cparams(("parallel", "arbitrary")),
        name="fox_cumsum",
    )(f_logits, b_f)


def _fox_attn_kernel(q_ref, kt_ref, v_ref, c_ref, o_ref, m_ref, l_ref, acc_ref, *, scale):
    bq = q_ref.shape[0]
    qi = pl.program_id(1)
    q = q_ref[...] * scale
    q_start = pl.multiple_of(qi * bq, bq)
    c0 = c_ref[:, pl.ds(q_start, bq)][:, 0:1]
    m_ref[...] = jnp.full_like(m_ref, -jnp.inf)
    l_ref[...] = jnp.zeros_like(l_ref)
    acc_ref[...] = jnp.zeros_like(acc_ref)

    def block(j, masked):
        k_start = pl.multiple_of(j * bq, bq)
        s = _dot(q, kt_ref[:, pl.ds(k_start, bq)]) + (c0 - c_ref[:, pl.ds(k_start, bq)])
        if masked:
            rr = lax.broadcasted_iota(jnp.int32, (bq, bq), 0)
            cc = lax.broadcasted_iota(jnp.int32, (bq, bq), 1)
            s = jnp.where(cc <= rr, s, -jnp.inf)
        m_old = m_ref[...]
        m_new = jnp.maximum(m_old, jnp.max(s, axis=-1, keepdims=True))
        p = jnp.exp(s - m_new)
        corr = jnp.exp(m_old - m_new)
        l_ref[...] = l_ref[...] * corr + jnp.sum(p, axis=-1, keepdims=True)
        acc_ref[...] = acc_ref[...] * corr + _dot(p.astype(BF16), v_ref[pl.ds(k_start, bq), :])
        m_ref[...] = m_new

    def body(j, carry):
        block(j, False)
        return carry

    lax.fori_loop(0, qi, body, 0)
    block(qi, True)
    o_ref[...] = (acc_ref[...] / l_ref[...]).astype(o_ref.dtype)


def fox_attention(q, kt, v, c, *, bq, out_dtype):
    bh, t_len, dh = q.shape
    scale = dh ** -0.5
    assert 2.0 ** round(math.log2(scale)) == scale
    return pl.pallas_call(
        functools.partial(_fox_attn_kernel, scale=scale),
        grid=(bh, t_len // bq),
        in_specs=[pl.BlockSpec((None, bq, dh), lambda b, i: (b, i, 0)),
                  pl.BlockSpec((None, dh, t_len), lambda b, i: (b, 0, 0)),
                  pl.BlockSpec((None, t_len, dh), lambda b, i: (b, 0, 0)),
                  pl.BlockSpec((None, 1, t_len), lambda b, i: (b, 0, 0))],
        out_specs=pl.BlockSpec((None, bq, dh), lambda b, i: (b, i, 0)),
        out_shape=jax.ShapeDtypeStruct((bh, t_len, dh), out_dtype),
        scratch_shapes=[pltpu.VMEM((bq, 1), F32), pltpu.VMEM((bq, 1), F32), pltpu.VMEM((bq, dh), F32)],
        compiler_params=_cparams(("parallel", "arbitrary")),
        name="fox_attention",
    )(q, kt, v, c)


def _mem_attn_kernel(q_ref, kt_ref, v_ref, o_ref, *, heads, scale):
    dh = q_ref.shape[1] // heads
    outs = []
    for h in range(heads):
        q = q_ref[:, h * dh:(h + 1) * dh]
        s = _dot(q, kt_ref[h]) * scale
        m = jnp.max(s, axis=-1, keepdims=True)
        p = jnp.exp(s - m)
        o = _dot(p.astype(BF16), v_ref[h]) / jnp.sum(p, axis=-1, keepdims=True)
        outs.append(o)
    o_ref[...] = jnp.concatenate(outs, axis=-1).astype(o_ref.dtype)


def mem_attention(qsrc, q_col, kt, v, *, heads, width, tq, out_dtype):
    bsz, t_len, _ = qsrc.shape
    dh = width // heads
    m_len = v.shape[2]
    return pl.pallas_call(
        functools.partial(_mem_attn_kernel, heads=heads, scale=dh ** -0.5),
        grid=(bsz, t_len // tq),
        in_specs=[pl.BlockSpec((None, tq, width), lambda b, i: (b, i, q_col)),
                  pl.BlockSpec((None, heads, dh, m_len), lambda b, i: (b, 0, 0, 0)),
                  pl.BlockSpec((None, heads, m_len, dh), lambda b, i: (b, 0, 0, 0))],
        out_specs=pl.BlockSpec((None, tq, width), lambda b, i: (b, i, 0)),
        out_shape=jax.ShapeDtypeStruct((bsz, t_len, width), out_dtype),
        compiler_params=_cparams(("parallel", "parallel")),
        name="mem_attention",
    )(qsrc, kt, v)


def _mix_out_kernel(yr_ref, yf_ref, ym_ref, x_ref, eg_ref, eb_ref, woa_ref, wob_ref, woc_ref,
                    g_ref, b_ref, rw_ref, rb_ref, h_ref, ti_ref, tg_ref, *, alpha, n_experts):
    bm, d = x_ref.shape
    nslab = d // LANES
    h0 = _layer_norm_rows(x_ref[...], eg_ref[...], eb_ref[...], LN_EPS)
    mixed = _dot(yr_ref[...], woa_ref[...]) + _dot(yf_ref[...], wob_ref[...]) + _dot(ym_ref[...], woc_ref[...])
    h1 = _layer_norm_rows(alpha * h0 + mixed, g_ref[...], b_ref[...], LN_EPS)
    for s in range(nslab):
        h_ref[pl.ds(s, bm, stride=nslab), :] = h1[:, s * LANES:(s + 1) * LANES]

    logits = _dot(h1, rw_ref[...], HIGHEST) + rb_ref[...]
    lane = lax.broadcasted_iota(jnp.int32, logits.shape, 1)
    work = jnp.where(lane < n_experts, logits, -jnp.inf)
    vals, idxs = [], []
    for _ in range(TOP_K):
        m = jnp.max(work, axis=-1, keepdims=True)
        idx = jnp.min(jnp.where(work == m, lane, LANES), axis=-1, keepdims=True)
        work = jnp.where(lane == idx, -jnp.inf, work)
        vals.append(m)
        idxs.append(idx)
    exps = [jnp.exp(v_ - vals[0]) for v_ in vals]
    denom = exps[0] + exps[1] + exps[2] + exps[3]
    ti = jnp.zeros(logits.shape, jnp.int32)
    tg = jnp.zeros(logits.shape, F32)
    for k_ in range(TOP_K):
        ti = jnp.where(lane == k_, idxs[k_], ti)
        tg = jnp.where(lane == k_, exps[k_] / denom, tg)
    ti_ref[...] = ti
    tg_ref[...] = tg


def mix_out(yr, yf, ym, x, emb_g, emb_b, wo, ln_g, ln_b, router_w, router_b, *, alpha, n_experts, bm):
    n, d = x.shape
    wr, wf, wm = yr.shape[1], yf.shape[1], ym.shape[1]
    assert wf == wm and wr % wf == 0 and d % LANES == 0
    nslab = d // LANES
    row = lambda w_: pl.BlockSpec((bm, w_), lambda i: (i, 0))
    const = lambda r_, w_, ri=0: pl.BlockSpec((r_, w_), lambda i: (ri, 0))
    return pl.pallas_call(
        functools.partial(_mix_out_kernel, alpha=alpha, n_experts=n_experts),
        grid=(n // bm,),
        in_specs=[row(wr), row(wf), row(wm), row(d), const(1, d), const(1, d),
                  const(wr, d), const(wf, d, wr // wf), const(wm, d, wr // wf + 1),
                  const(1, d), const(1, d), const(d, LANES), const(1, LANES)],
        out_specs=[pl.BlockSpec((bm * nslab, LANES), lambda i: (i, 0)), row(LANES), row(LANES)],
        out_shape=[jax.ShapeDtypeStruct((n * nslab, LANES), F32),
                   jax.ShapeDtypeStruct((n, LANES), jnp.int32),
                   jax.ShapeDtypeStruct((n, LANES), F32)],
        compiler_params=_cparams(("parallel",)),
        name="mix_out",
    )(yr, yf, ym, x, emb_g.reshape(1, d), emb_b.reshape(1, d), wo, wo, wo,
      ln_g.reshape(1, d), ln_b.reshape(1, d), router_w, router_b)


def _moe_kernel(be_ref, nu_ref, tok_ref, h_ref, w1g_ref, w1l_ref, b1g_ref, b1l_ref, w2_ref, b2_ref,
                o_ref, idx_ref, xraw_ref, xb_ref, acc_ref, idx_sem, row_sem, *, nslab):
    bm = xb_ref.shape[0]
    i = pl.program_id(0)
    f = pl.program_id(1)
    nf = pl.num_programs(1)
    n_used = nu_ref[0]

    def fetch_rows(blk):
        cp = pltpu.make_async_copy(tok_ref.at[blk], idx_ref, idx_sem)
        cp.start()
        cp.wait()

        def issue(r, carry):
            src = pl.multiple_of(idx_ref[r] * nslab, nslab)
            dst = pl.multiple_of(r * nslab, nslab)
            pltpu.make_async_copy(h_ref.at[pl.ds(src, nslab), :], xraw_ref.at[pl.ds(dst, nslab), :], row_sem).start()
            return carry

        lax.fori_loop(0, bm, issue, 0)

    @pl.when(i < n_used)
    def _():
        @pl.when(f == 0)
        def _():
            @pl.when(i == 0)
            def _():
                fetch_rows(0)

            pltpu.make_async_copy(h_ref.at[pl.ds(0, bm * nslab), :], xraw_ref, row_sem).wait()
            for s in range(nslab):
                xb_ref[:, s * LANES:(s + 1) * LANES] = xraw_ref[pl.ds(s, bm, stride=nslab), :].astype(BF16)

            @pl.when(i + 1 < n_used)
            def _():
                fetch_rows(i + 1)

        xb = xb_ref[...]
        hg = _dot(xb, w1g_ref[...].astype(BF16)) + b1g_ref[...]
        hl = _dot(xb, w1l_ref[...].astype(BF16)) + b1l_ref[...]
        glu = jnp.minimum(hg, SWIGLU_LIMIT)
        lin = jnp.clip(hl, -SWIGLU_LIMIT, SWIGLU_LIMIT)
        act = (lin + 1.0) * glu * _sigmoid(SWIGLU_ALPHA * glu)
        part = _dot(act.astype(BF16), w2_ref[...].astype(BF16))

        @pl.when(f == 0)
        def _():
            acc_ref[...] = part

        @pl.when(f > 0)
        def _():
            acc_ref[...] += part

        @pl.when(f == nf - 1)
        def _():
            for s in range(nslab):
                sl = slice(s * LANES, (s + 1) * LANES)
                o_ref[pl.ds(s, bm, stride=nslab), :] = acc_ref[:, sl] + b2_ref[:, sl]

    @pl.when((i >= n_used) & (f == nf - 1))
    def _():
        o_ref[...] = jnp.zeros_like(o_ref)


def moe_experts(block_e, n_used, tok_blocks, h_rows, w1, b1, w2, b2, *, bm, tf):
    nblk = tok_blocks.shape[0]
    n_exp, d, de2 = w1.shape
    de = de2 // 2
    nslab = d // LANES
    nf = de // tf

    def blk(i, nu):
        return jnp.minimum(i, nu[0] - 1)

    def fi(i, f, nu):
        return jnp.where(i < nu[0], f, nf - 1)

    grid_spec = pltpu.PrefetchScalarGridSpec(
        num_scalar_prefetch=2,
        grid=(nblk, nf),
        in_specs=[
            pl.BlockSpec(memory_space=pl.ANY),
            pl.BlockSpec(memory_space=pl.ANY),
            pl.BlockSpec((None, d, tf), lambda i, f, be, nu: (be[blk(i, nu)], 0, fi(i, f, nu))),
            pl.BlockSpec((None, d, tf), lambda i, f, be, nu: (be[blk(i, nu)], 0, nf + fi(i, f, nu))),
            pl.BlockSpec((None, 1, tf), lambda i, f, be, nu: (be[blk(i, nu)], 0, fi(i, f, nu))),
            pl.BlockSpec((None, 1, tf), lambda i, f, be, nu: (be[blk(i, nu)], 0, nf + fi(i, f, nu))),
            pl.BlockSpec((None, tf, d), lambda i, f, be, nu: (be[blk(i, nu)], fi(i, f, nu), 0)),
            pl.BlockSpec((None, 1, d), lambda i, f, be, nu: (be[blk(i, nu)], 0, 0)),
        ],
        out_specs=pl.BlockSpec((bm * nslab, LANES), lambda i, f, be, nu: (i, 0)),
        scratch_shapes=[
            pltpu.SMEM((bm,), jnp.int32),
            pltpu.VMEM((bm * nslab, LANES), F32),
            pltpu.VMEM((bm, d), BF16),
            pltpu.VMEM((bm, d), F32),
            pltpu.SemaphoreType.DMA,
            pltpu.SemaphoreType.DMA,
        ],
    )
    return pl.pallas_call(
        functools.partial(_moe_kernel, nslab=nslab),
        grid_spec=grid_spec,
        out_shape=jax.ShapeDtypeStruct((nblk * bm * nslab, LANES), F32),
        compiler_params=_cparams(("arbitrary", "arbitrary")),
        name="moe_experts",
    )(block_e, n_used, tok_blocks, h_rows, w1, w1, b1, b1, w2, b2)


def _combine_kernel(pos_ref, y_ref, h_ref, tg_ref, g_ref, b_ref, o_ref, idx_ref, ybuf_ref, idx_sem, row_sem,
                    *, alpha, nslab):
    tm = o_ref.shape[0]
    nrow = TOP_K * tm
    i = pl.program_id(0)
    n_steps = pl.num_programs(0)

    def fetch_rows(step, slot):
        cp = pltpu.make_async_copy(pos_ref.at[step], idx_ref.at[slot], idx_sem)
        cp.start()
        cp.wait()

        def issue(r, carry):
            src = pl.multiple_of(idx_ref[slot, r] * nslab, nslab)
            dst = pl.multiple_of(r * nslab, nslab)
            pltpu.make_async_copy(y_ref.at[pl.ds(src, nslab), :], ybuf_ref.at[slot, pl.ds(dst, nslab), :],
                                  row_sem.at[slot]).start()
            return carry

        lax.fori_loop(0, nrow, issue, 0)

    slot = i % 2

    @pl.when(i == 0)
    def _():
        fetch_rows(0, 0)

    @pl.when(i + 1 < n_steps)
    def _():
        fetch_rows(i + 1, 1 - slot)

    pltpu.make_async_copy(y_ref.at[pl.ds(0, nrow * nslab), :], ybuf_ref.at[slot], row_sem.at[slot]).wait()

    gates = [tg_ref[:, k_:k_ + 1] for k_ in range(TOP_K)]
    slabs = []
    for s in range(nslab):
        acc = alpha * h_ref[pl.ds(s, tm, stride=nslab), :]
        for k_ in range(TOP_K):
            acc = acc + gates[k_] * ybuf_ref[slot, pl.ds(k_ * tm * nslab + s, tm, stride=nslab), :]
        slabs.append(acc)
    pre = jnp.concatenate(slabs, axis=-1)
    o_ref[...] = _layer_norm_rows(pre, g_ref[...], b_ref[...], LN_EPS).astype(o_ref.dtype)


def moe_combine(pos_blocks, y_rows, h_rows, gates, ln_g, ln_b, *, alpha, tm, out_dtype):
    n = gates.shape[0]
    d = ln_g.shape[-1]
    nslab = d // LANES
    nrow = TOP_K * tm
    return pl.pallas_call(
        functools.partial(_combine_kernel, alpha=alpha, nslab=nslab),
        grid=(n // tm,),
        in_specs=[pl.BlockSpec(memory_space=pl.ANY),
                  pl.BlockSpec(memory_space=pl.ANY),
                  pl.BlockSpec((tm * nslab, LANES), lambda i: (i, 0)),
                  pl.BlockSpec((tm, LANES), lambda i: (i, 0)),
                  pl.BlockSpec((1, d), lambda i: (0, 0)),
                  pl.BlockSpec((1, d), lambda i: (0, 0))],
        out_specs=pl.BlockSpec((tm, d), lambda i: (i, 0)),
        out_shape=jax.ShapeDtypeStruct((n, d), out_dtype),
        scratch_shapes=[pltpu.SMEM((2, nrow), jnp.int32),
                        pltpu.VMEM((2, nrow * nslab, LANES), F32),
                        pltpu.SemaphoreType.DMA,
                        pltpu.SemaphoreType.DMA((2,))],
        compiler_params=_cparams(("arbitrary",)),
        name="moe_combine",
    )(pos_blocks, y_rows, h_rows, gates, ln_g.reshape(1, d), ln_b.reshape(1, d))


def _route(top_idx, n_experts, bm, tm):
    n = top_idx.shape[0]
    n_assign = n * TOP_K
    e = top_idx.reshape(-1)
    onehot = (e[:, None] == jnp.arange(n_experts, dtype=jnp.int32)[None, :]).astype(jnp.int32)
    csum = jnp.cumsum(onehot, axis=0)
    rank = jnp.sum((csum - onehot) * onehot, axis=-1)
    counts = csum[-1]
    padded = ((counts + bm - 1) // bm) * bm
    pend = jnp.cumsum(padded)
    pstart = pend - padded
    pos = jnp.sum(onehot * pstart[None, :], axis=-1) + rank
    nblk = n_assign // bm + n_experts
    tok = jnp.arange(n_assign, dtype=jnp.int32) // TOP_K
    buf_tok = jnp.zeros((nblk * bm,), jnp.int32).at[pos].set(tok)
    block_e = jnp.clip(jnp.searchsorted(pend, jnp.arange(nblk, dtype=jnp.int32) * bm, side="right"),
                       0, n_experts - 1).astype(jnp.int32)
    n_used = (pend[-1] // bm).astype(jnp.int32).reshape(1)
    pos_blocks = pos.reshape(n // tm, tm, TOP_K).transpose(0, 2, 1).reshape(n // tm, TOP_K * tm)
    return block_e, n_used, buf_tok.reshape(nblk, bm), pos_blocks.astype(jnp.int32)


RWKV_HEADS = 16
DECAY_LORA = 96
ICLR_LORA = 96
GATE_LORA = 256
FOX_HEADS = 8
MEM_HEADS = 4
MEM_HEAD_DIM = 128
N_EXPERTS = 32
LORA_PAD = 128

TILES = dict(proj_bm=1024, proj_bn=896, qkvm_bn=1024, rwkv_tt=1024, fox_tt=512, fox_bq=512, mem_tq=1024,
             mix_bm=512, moe_bm=512, moe_tf=256, comb_tm=256)


def _pad_cols(a, width):
    return jnp.pad(a, [(0, 0)] * (a.ndim - 1) + [(0, width - a.shape[-1])])


def _layer(x, mem_n_kv, emb_ln_g, emb_ln_b, w_in, mu, w0, wd, a0, wa, wg, k_k, k_a, r_k, gn_g, gn_b, b_f,
           w_o, ln1_g, ln1_b, router_w, router_b, w1, b1, w2, b2, ln2_g, ln2_b, alpha, tiles):
    bsz, t_len, d = x.shape
    n = bsz * t_len
    wr = RWKV_HEADS * HEAD64
    wf = FOX_HEADS * HEAD64
    wm = MEM_HEADS * MEM_HEAD_DIM
    c0 = 3 * wr
    c1 = c0 + DECAY_LORA
    c2 = c1 + ICLR_LORA
    c3 = c2 + GATE_LORA
    f0 = c3 + 3 * wf
    f1 = f0 + FOX_HEADS
    tl = lambda name, dim: min(tiles[name], dim)

    def rwkv_cols(a):
        return jnp.concatenate([a[..., :c0], _pad_cols(a[..., c0:c1], LORA_PAD),
                                _pad_cols(a[..., c1:c2], LORA_PAD), a[..., c2:c3]], axis=-1)

    w_rwkv = rwkv_cols(w_in).astype(BF16)
    w_qkvm = jnp.concatenate([w_in[:, c3:f0], w_in[:, f1:]], axis=-1).astype(BF16)
    w_f = _pad_cols(w_in[:, f0:f1], LANES).astype(BF16)
    x2 = x.reshape(n, d)
    bm = tl("proj_bm", n)
    p_rwkv = ln_matmul(x2, emb_ln_g, emb_ln_b, w_rwkv, F32, bm, tl("proj_bn", w_rwkv.shape[1]))
    p_qkvm = ln_matmul(x2, emb_ln_g, emb_ln_b, w_qkvm, BF16, bm, tl("qkvm_bn", w_qkvm.shape[1]))
    f_log = ln_matmul(x2, emb_ln_g, emb_ln_b, w_f, F32, bm, LANES)

    zrows = lambda a, rows: jnp.pad(a, ((0, rows - a.shape[0]), (0, 0)))
    y_rwkv = rwkv_mix(p_rwkv.reshape(bsz, t_len, -1), rwkv_cols(mu), w0, a0, k_k, k_a, r_k.reshape(-1), gn_g, gn_b,
                      zrows(wd, LORA_PAD), zrows(wa, LORA_PAD), wg,
                      width=wr, tt=tl("rwkv_tt", t_len), out_dtype=BF16)

    c = fox_cumsum(f_log.reshape(bsz, t_len, LANES), _pad_cols(b_f.reshape(1, -1), LANES), tl("fox_tt", t_len))
    c_rows = c[..., :FOX_HEADS].transpose(0, 2, 1).reshape(bsz * FOX_HEADS, 1, t_len)
    qkv = p_qkvm[:, :3 * wf].reshape(bsz, t_len, 3, FOX_HEADS, HEAD64)
    qkv = qkv.transpose(2, 0, 3, 1, 4).reshape(3, bsz * FOX_HEADS, t_len, HEAD64)
    o_fox = fox_attention(qkv[0], qkv[1].transpose(0, 2, 1), qkv[2], c_rows, bq=tl("fox_bq", t_len), out_dtype=BF16)
    y_fox = o_fox.reshape(bsz, FOX_HEADS, t_len, HEAD64).transpose(0, 2, 1, 3).reshape(n, wf)

    m_len = mem_n_kv.shape[0] // bsz
    km = mem_n_kv[:, :wm].reshape(bsz, m_len, MEM_HEADS, MEM_HEAD_DIM).transpose(0, 2, 3, 1).astype(BF16)
    vm = mem_n_kv[:, wm:].reshape(bsz, m_len, MEM_HEADS, MEM_HEAD_DIM).transpose(0, 2, 1, 3).astype(BF16)
    y_mem = mem_attention(p_qkvm.reshape(bsz, t_len, -1), (3 * wf) // wm, km, vm, heads=MEM_HEADS, width=wm,
                          tq=tl("mem_tq", t_len), out_dtype=BF16)

    rw = _pad_cols(router_w, LANES)
    rb = _pad_cols(router_b.reshape(1, -1), LANES)
    h_rows, top_i, top_g = mix_out(y_rwkv.reshape(n, wr), y_fox, y_mem.reshape(n, wm), x2, emb_ln_g, emb_ln_b,
                                   w_o.astype(BF16), ln1_g, ln1_b, rw, rb, alpha=alpha, n_experts=N_EXPERTS,
                                   bm=tl("mix_bm", n))

    moe_bm = tl("moe_bm", n)
    tm = tl("comb_tm", n)
    block_e, n_used, tok_blocks, pos_blocks = _route(top_i[:, :TOP_K], N_EXPERTS, moe_bm, tm)
    y_rows = moe_experts(block_e, n_used, tok_blocks, h_rows, w1, b1[:, None, :], w2, b2[:, None, :],
                         bm=moe_bm, tf=tl("moe_tf", w2.shape[1]))
    out = moe_combine(pos_blocks, y_rows, h_rows, top_g, ln2_g, ln2_b, alpha=alpha, tm=tm, out_dtype=x.dtype)
    return out.reshape(bsz, t_len, d)


def kernel(x, mem, emb_ln_g, emb_ln_b, w_in, rwkv_mu, rwkv_w0, rwkv_w_decay_up, rwkv_a0, rwkv_w_iclr_up,
           rwkv_w_gate_up, rwkv_k_k, rwkv_k_a, rwkv_r_k, rwkv_gn_g, rwkv_gn_b, fox_b_f, mem_ln_g, mem_ln_b,
           w_mem_kv, w_o, ln1_g, ln1_b, router_w, router_b, expert_w1, expert_b1, expert_w2, expert_b2,
           ln2_g, ln2_b):
    depth = w_in.shape[0]
    assert depth == 1
    alpha = (2 * depth) ** 0.25
    bsz, m_len, d = mem.shape
    kv = ln_matmul(mem.reshape(bsz * m_len, d), mem_ln_g, mem_ln_b, w_mem_kv[0].astype(BF16), F32,
                   min(512, bsz * m_len), w_mem_kv.shape[-1])
    return _layer(x, kv, emb_ln_g, emb_ln_b, w_in[0], rwkv_mu[0], rwkv_w0[0], rwkv_w_decay_up[0], rwkv_a0[0],
                  rwkv_w_iclr_up[0], rwkv_w_gate_up[0], rwkv_k_k[0], rwkv_k_a[0], rwkv_r_k[0], rwkv_gn_g[0],
                  rwkv_gn_b[0], fox_b_f[0], w_o[0], ln1_g[0], ln1_b[0], router_w[0], router_b[0],
                  expert_w1[0], expert_b1[0], expert_w2[0], expert_b2[0], ln2_g[0], ln2_b[0], alpha, TILES)
```

```python
import functools
import math

import jax
import jax.numpy as jnp
from jax import lax
from jax.experimental import pallas as pl
from jax.experimental.pallas import tpu as pltpu

F32 = jnp.float32
BF16 = jnp.bfloat16
HIGHEST = lax.Precision.HIGHEST

LANES = 128
HEAD64 = 64
VMEM_LIMIT = 56 * 1024 * 1024

LN_EPS = 1e-5
RWKV_GN_EPS = 64e-5
SWIGLU_LIMIT = 7.0
SWIGLU_ALPHA = 1.702
TOP_K = 4


def _cparams(sem):
    return pltpu.CompilerParams(dimension_semantics=sem, vmem_limit_bytes=VMEM_LIMIT)


def _dot(a, b, precision=None):
    return jnp.dot(a, b, preferred_element_type=F32, precision=precision)


def _dot_nt(a, b, precision=None):
    return lax.dot_general(a, b, (((1,), (1,)), ((), ())), preferred_element_type=F32, precision=precision)


def _dot_tn(a, b, precision=None):
    return lax.dot_general(a, b, (((0,), (0,)), ((), ())), preferred_element_type=F32, precision=precision)


def _dot_split(a, b, split_rhs=False):
    x = b if split_rhs else a
    hi = x.astype(BF16)
    lo = (x - hi.astype(F32)).astype(BF16)
    if split_rhs:
        return _dot(a, hi) + _dot(a, lo)
    return _dot(hi, b) + _dot(lo, b)


def _layer_norm_rows(x, g, b, eps):
    mean = jnp.mean(x, axis=-1, keepdims=True)
    xc = x - mean
    var = jnp.mean(xc * xc, axis=-1, keepdims=True)
    return xc * lax.rsqrt(var + eps) * g + b


def _sigmoid(x):
    return 1.0 / (1.0 + jnp.exp(-x))


def _ln_matmul_kernel(x_ref, g_ref, b_ref, w_ref, o_ref, xn_ref):
    @pl.when(pl.program_id(1) == 0)
    def _():
        xn_ref[...] = _layer_norm_rows(x_ref[...], g_ref[...], b_ref[...], LN_EPS).astype(BF16)

    o_ref[...] = _dot(xn_ref[...], w_ref[...]).astype(o_ref.dtype)


def ln_matmul(x, g, b, w, out_dtype, bm, bn):
    n, d = x.shape
    m = w.shape[1]
    assert n % bm == 0 and m % bn == 0
    return pl.pallas_call(
        _ln_matmul_kernel,
        grid=(n // bm, m // bn),
        in_specs=[
            pl.BlockSpec((bm, d), lambda i, j: (i, 0)),
            pl.BlockSpec((1, d), lambda i, j: (0, 0)),
            pl.BlockSpec((1, d), lambda i, j: (0, 0)),
            pl.BlockSpec((d, bn), lambda i, j: (0, j)),
        ],
        out_specs=pl.BlockSpec((bm, bn), lambda i, j: (i, j)),
        out_shape=jax.ShapeDtypeStruct((n, m), out_dtype),
        scratch_shapes=[pltpu.VMEM((bm, d), BF16)],
        compiler_params=_cparams(("parallel", "arbitrary")),
        name="ln_matmul",
    )(x, g.reshape(1, d), b.reshape(1, d), w)


RWKV_CHUNK = 64


def _rwkv_kernel(r_ref, k_ref, v_ref, dw_ref, da_ref, dg_ref,
                 mur_ref, muk_ref, muv_ref, mudw_ref, muda_ref, mudg_ref,
                 w0_ref, a0_ref, kk_ref, ka_ref, rk_ref, gng_ref, gnb_ref,
                 wd_ref, wa_ref, wg_ref,
                 o_ref,
                 cr_ref, ck_ref, cv_ref, cdw_ref, cda_ref, cdg_ref, h_ref,
                 sr_ref, slw_ref, sk_ref, sv_ref, sa_ref, sb_ref, sy_ref,
                 rp_ref, y0_ref, m_ref, ha_ref):
    tt = r_ref.shape[0]
    C = RWKV_CHUNK
    C2 = 2 * C

    @pl.when(pl.program_id(2) == 0)
    def _():
        for c in (cr_ref, ck_ref, cv_ref, cdw_ref, cda_ref, cdg_ref, h_ref):
            c[...] = jnp.zeros_like(c)

    def shift(x_ref, mu_ref, carry_ref):
        x = x_ref[...]
        rows = lax.broadcasted_iota(jnp.int32, x.shape, 0)
        prev = jnp.where(rows == 0, carry_ref[0:1, :], pltpu.roll(x, 1, axis=0))
        carry_ref[0:1, :] = x[tt - 1:tt, :]
        return x + mu_ref[...] * (prev - x)

    r = shift(r_ref, mur_ref, cr_ref)
    k = shift(k_ref, muk_ref, ck_ref)
    v = shift(v_ref, muv_ref, cv_ref)
    dw = shift(dw_ref, mudw_ref, cdw_ref)
    da = shift(da_ref, muda_ref, cda_ref)
    dg = shift(dg_ref, mudg_ref, cdg_ref)

    lane = lax.broadcasted_iota(jnp.int32, (LANES, LANES), 1)
    row = lax.broadcasted_iota(jnp.int32, (LANES, LANES), 0)
    seg_ones = jnp.where((lane // HEAD64) == (row // HEAD64), 1.0, 0.0).astype(BF16)

    def segsum(x):
        return _dot_split(x, seg_ones)

    bdot = lambda a_, b_: _dot(a_.astype(BF16), b_.astype(BF16))
    z = w0_ref[...] + bdot(jnp.tanh(dw), wd_ref[...])
    softplus_neg = jnp.maximum(-z, 0.0) + jnp.log(1.0 + jnp.exp(-jnp.abs(z)))
    lw = -jnp.exp(-softplus_neg - 0.5)
    a = _sigmoid(a0_ref[...] + bdot(da, wa_ref[...]))
    gate = bdot(_sigmoid(dg), wg_ref[...])
    kk = k * kk_ref[...]
    kk = kk / jnp.maximum(jnp.sqrt(segsum(kk * kk)), 1e-12)
    k2 = k * (1.0 + (a - 1.0) * ka_ref[...])
    bonus = segsum(r * k2 * rk_ref[...]) * v

    sr_ref[...] = r
    slw_ref[...] = lw
    sk_ref[...] = k2
    sv_ref[...] = v
    sa_ref[...] = -kk
    sb_ref[...] = kk * a

    lane_c = lax.broadcasted_iota(jnp.int32, (C, LANES), 1)
    head0 = lane_c < HEAD64
    rr = lax.broadcasted_iota(jnp.int32, (C2, C2), 0)
    cc = lax.broadcasted_iota(jnp.int32, (C2, C2), 1)
    same_head = (rr // C) == (cc // C)
    strict = same_head & ((rr % C) > (cc % C))
    incl = same_head & ((rr % C) >= (cc % C))
    eye2 = jnp.where(rr == cc, 1.0, 0.0).astype(F32)
    rc = lax.broadcasted_iota(jnp.int32, (C, C), 0)
    cc1 = lax.broadcasted_iota(jnp.int32, (C, C), 1)
    tri_incl = jnp.where(rc >= cc1, 1.0, 0.0).astype(BF16)
    diag_mask = row == lane

    def stack(x):
        return jnp.concatenate([jnp.where(head0, x, 0.0), jnp.where(head0, 0.0, x)], axis=0)

    def chunk_terms(c, carry):
        sl = pl.ds(pl.multiple_of(c * C, C), C)
        sq = pl.ds(pl.multiple_of(c * C2, C2), C2)
        rc_, lwc, kc, vc, ac, bc = sr_ref[sl, :], slw_ref[sl, :], sk_ref[sl, :], sv_ref[sl, :], sa_ref[sl, :], sb_ref[sl, :]
        lg = _dot_split(tri_incl, lwc, split_rhs=True)
        lg_last = lg[C - 1:C, :]
        e_pos = jnp.exp(lg)
        e_neg = jnp.exp(-lg)
        e_end = jnp.exp(lg_last - lg)
        rt = stack(rc_ * e_pos)
        at = stack(ac * jnp.exp(lg - lwc)).astype(BF16)
        rtb = rt.astype(BF16)
        bt = stack(bc * e_neg).astype(BF16)
        kt = stack(kc * e_neg).astype(BF16)
        bh = stack(bc * e_end).astype(BF16)
        kh = stack(kc * e_end).astype(BF16)
        vs = stack(vc).astype(BF16)

        a_ab = jnp.where(strict, _dot_nt(at, bt), 0.0)
        a_ak = jnp.where(strict, _dot_nt(at, kt), 0.0).astype(BF16)
        a_rb = jnp.where(incl, _dot_nt(rtb, bt), 0.0).astype(BF16)
        a_rk = jnp.where(incl, _dot_nt(rtb, kt), 0.0).astype(BF16)

        inv = eye2 + a_ab
        pw = a_ab
        for _ in range(int(math.log2(C)) - 1):
            pwb = pw.astype(BF16)
            pw = _dot(pwb, pwb)
            inv = inv + _dot(inv.astype(BF16), pw.astype(BF16))
        invb = inv.astype(BF16)

        atp = _dot(invb, at).astype(BF16)
        u0 = _dot(invb, _dot(a_ak, vs).astype(BF16)).astype(BF16)
        rp_ref[sq, :] = rt + _dot(a_rb, atp)
        y0_ref[sq, :] = _dot(a_rb, u0) + _dot(a_rk, vs)
        m_ref[sq, :] = jnp.where(diag_mask, jnp.exp(lg_last), 0.0) + _dot_tn(bh, atp)
        ha_ref[sq, :] = _dot_tn(bh, u0) + _dot_tn(kh, vs)
        return carry

    lax.fori_loop(0, tt // C, chunk_terms, 0, unroll=2)

    def chunk_state(c, carry):
        sl = pl.ds(pl.multiple_of(c * C, C), C)
        sq = pl.ds(pl.multiple_of(c * C2, C2), C2)
        h0 = h_ref[...]
        y = y0_ref[sq, :] + _dot(rp_ref[sq, :], h0, HIGHEST)
        sy_ref[sl, :] = y[0:C, :] + y[C:C2, :]
        h_ref[...] = ha_ref[sq, :] + _dot(m_ref[sq, :], h0, HIGHEST)
        return carry

    lax.fori_loop(0, tt // C, chunk_state, 0)

    y = sy_ref[...]
    mean = segsum(y) * (1.0 / HEAD64)
    yc = y - mean
    var = segsum(yc * yc) * (1.0 / HEAD64)
    yn = yc * lax.rsqrt(var + RWKV_GN_EPS) * gng_ref[...] + gnb_ref[...]
    o_ref[...] = ((yn + bonus) * gate).astype(o_ref.dtype)


def rwkv_mix(p, mu, w0, a0, k_k, k_a, r_k, gn_g, gn_b, wd, wa, wg, *, width, tt, out_dtype):
    bsz, t_len, _ = p.shape
    assert width % LANES == 0 and (3 * width) % 256 == 0 and t_len % tt == 0 and tt % RWKV_CHUNK == 0
    npair = width // LANES
    nw = width // LANES
    col_dw = 3 * nw
    col_da = 3 * nw + 1
    col_dg = (3 * width + 256) // 256

    def tok(width_, col):
        return pl.BlockSpec((None, tt, width_), lambda b, h, t: (b, t, col(h)))

    def par(rows, width_, col):
        return pl.BlockSpec((rows, width_), lambda b, h, t: (0, col(h)))

    vec = lambda a_: a_.reshape(1, -1)
    cols = [lambda h: h, lambda h: nw + h, lambda h: 2 * nw + h,
            lambda h: col_dw, lambda h: col_da, lambda h: col_dg]
    widths = [LANES, LANES, LANES, LANES, LANES, 256]
    in_specs = [tok(w_, c_) for w_, c_ in zip(widths, cols)]
    in_specs += [par(1, w_, c_) for w_, c_ in zip(widths, cols)]
    in_specs += [par(1, LANES, lambda h: h)] * 7
    in_specs += [par(LANES, LANES, lambda h: h), par(LANES, LANES, lambda h: h), par(256, LANES, lambda h: h)]
    mu2 = vec(mu)
    args = [p] * 6 + [mu2] * 6 + [vec(w0), vec(a0), vec(k_k), vec(k_a), vec(r_k), vec(gn_g), vec(gn_b), wd, wa, wg]
    carry = lambda w_: pltpu.VMEM((8, w_), F32)
    tile = pltpu.VMEM((tt, LANES), F32)
    return pl.pallas_call(
        _rwkv_kernel,
        grid=(bsz, npair, t_len // tt),
        in_specs=in_specs,
        out_specs=pl.BlockSpec((None, tt, LANES), lambda b, h, t: (b, t, h)),
        out_shape=jax.ShapeDtypeStruct((bsz, t_len, width), out_dtype),
        scratch_shapes=([carry(LANES)] * 5 + [carry(256), pltpu.VMEM((LANES, LANES), F32)] + [tile] * 7
                        + [pltpu.VMEM((2 * tt, LANES), F32)] * 4),
        compiler_params=_cparams(("parallel", "parallel", "arbitrary")),
        name="rwkv_mix",
    )(*args)


def _fox_cumsum_kernel(f_ref, bf_ref, o_ref, carry_ref):
    @pl.when(pl.program_id(1) == 0)
    def _():
        carry_ref[...] = jnp.zeros_like(carry_ref)

    tt = f_ref.shape[0]
    z = f_ref[...] + bf_ref[...]
    log_f = -(jnp.maximum(-z, 0.0) + jnp.log(1.0 + jnp.exp(-jnp.abs(z))))
    rr = lax.broadcasted_iota(jnp.int32, (tt, tt), 0)
    cc = lax.broadcasted_iota(jnp.int32, (tt, tt), 1)
    tri = jnp.where(rr >= cc, 1.0, 0.0).astype(F32)
    c = _dot(tri, log_f, HIGHEST) + carry_ref[0:1, :]
    o_ref[...] = c
    carry_ref[0:1, :] = c[tt - 1:tt, :]


def fox_cumsum(f_logits, b_f, tt):
    bsz, t_len, _ = f_logits.shape
    return pl.pallas_call(
        _fox_cumsum_kernel,
        grid=(bsz, t_len // tt),
        in_specs=[pl.BlockSpec((None, tt, LANES), lambda b, t: (b, t, 0)),
                  pl.BlockSpec((1, LANES), lambda b, t: (0, 0))],
        out_specs=pl.BlockSpec((None, tt, LANES), lambda b, t: (b, t, 0)),
        out_shape=jax.ShapeDtypeStruct((bsz, t_len, LANES), F32),
        scratch_shapes=[pltpu.VMEM((8, LANES), F32)],
        compiler_params=_cparams(("parallel", "arbitrary")),
        name="fox_cumsum",
    )(f_logits, b_f)


def _fox_attn_kernel(q_ref, kt_ref, v_ref, c_ref, o_ref, m_ref, l_ref, acc_ref, *, scale):
    bq = q_ref.shape[0]
    qi = pl.program_id(1)
    q = q_ref[...] * scale
    q_start = pl.multiple_of(qi * bq, bq)
    c0 = c_ref[:, pl.ds(q_start, bq)][:, 0:1]
    m_ref[...] = jnp.full_like(m_ref, -jnp.inf)
    l_ref[...] = jnp.zeros_like(l_ref)
    acc_ref[...] = jnp.zeros_like(acc_ref)

    def block(j, masked):
        k_start = pl.multiple_of(j * bq, bq)
        s = _dot(q, kt_ref[:, pl.ds(k_start, bq)]) + (c0 - c_ref[:, pl.ds(k_start, bq)])
        if masked:
            rr = lax.broadcasted_iota(jnp.int32, (bq, bq), 0)
            cc = lax.broadcasted_iota(jnp.int32, (bq, bq), 1)
            s = jnp.where(cc <= rr, s, -jnp.inf)
        m_old = m_ref[...]
        m_new = jnp.maximum(m_old, jnp.max(s, axis=-1, keepdims=True))
        p = jnp.exp(s - m_new)
        corr = jnp.exp(m_old - m_new)
        l_ref[...] = l_ref[...] * corr + jnp.sum(p, axis=-1, keepdims=True)
        acc_ref[...] = acc_ref[...] * corr + _dot(p.astype(BF16), v_ref[pl.ds(k_start, bq), :])
        m_ref[...] = m_new

    def body(j, carry):
        block(j, False)
        return carry

    lax.fori_loop(0, qi, body, 0)
    block(qi, True)
    o_ref[...] = (acc_ref[...] / l_ref[...]).astype(o_ref.dtype)


def fox_attention(q, kt, v, c, *, bq, out_dtype):
    bh, t_len, dh = q.shape
    scale = dh ** -0.5
    assert 2.0 ** round(math.log2(scale)) == scale
    return pl.pallas_call(
        functools.partial(_fox_attn_kernel, scale=scale),
        grid=(bh, t_len // bq),
        in_specs=[pl.BlockSpec((None, bq, dh), lambda b, i: (b, i, 0)),
                  pl.BlockSpec((None, dh, t_len), lambda b, i: (b, 0, 0)),
                  pl.BlockSpec((None, t_len, dh), lambda b, i: (b, 0, 0)),
                  pl.BlockSpec((None, 1, t_len), lambda b, i: (b, 0, 0))],
        out_specs=pl.BlockSpec((None, bq, dh), lambda b, i: (b, i, 0)),
        out_shape=jax.ShapeDtypeStruct((bh, t_len, dh), out_dtype),
        scratch_shapes=[pltpu.VMEM((bq, 1), F32), pltpu.VMEM((bq, 1), F32), pltpu.VMEM((bq, dh), F32)],
        compiler_params=_cparams(("parallel", "arbitrary")),
        name="fox_attention",
    )(q, kt, v, c)


def _mem_attn_kernel(q_ref, kt_ref, v_ref, o_ref, *, heads, scale):
    dh = q_ref.shape[1] // heads
    outs = []
    for h in range(heads):
        q = q_ref[:, h * dh:(h + 1) * dh]
        s = _dot(q, kt_ref[h]) * scale
        m = jnp.max(s, axis=-1, keepdims=True)
        p = jnp.exp(s - m)
        o = _dot(p.astype(BF16), v_ref[h]) / jnp.sum(p, axis=-1, keepdims=True)
        outs.append(o)
    o_ref[...] = jnp.concatenate(outs, axis=-1).astype(o_ref.dtype)


def mem_attention(qsrc, q_col, kt, v, *, heads, width, tq, out_dtype):
    bsz, t_len, _ = qsrc.shape
    dh = width // heads
    m_len = v.shape[2]
    return pl.pallas_call(
        functools.partial(_mem_attn_kernel, heads=heads, scale=dh ** -0.5),
        grid=(bsz, t_len // tq),
        in_specs=[pl.BlockSpec((None, tq, width), lambda b, i: (b, i, q_col)),
                  pl.BlockSpec((None, heads, dh, m_len), lambda b, i: (b, 0, 0, 0)),
                  pl.BlockSpec((None, heads, m_len, dh), lambda b, i: (b, 0, 0, 0))],
        out_specs=pl.BlockSpec((None, tq, width), lambda b, i: (b, i, 0)),
        out_shape=jax.ShapeDtypeStruct((bsz, t_len, width), out_dtype),
        compiler_params=_cparams(("parallel", "parallel")),
        name="mem_attention",
    )(qsrc, kt, v)


def _pack_bf16_pair(lo, hi):
    lo_bits = lax.bitcast_convert_type(lo.astype(BF16).astype(F32), jnp.uint32)
    hi_bits = lax.bitcast_convert_type(hi.astype(BF16).astype(F32), jnp.uint32)
    return lax.shift_right_logical(lo_bits, jnp.uint32(16)) | (hi_bits & jnp.uint32(0xFFFF0000))


def _unpack_bf16_pair(u):
    lo = lax.bitcast_convert_type(lax.shift_left(u, jnp.uint32(16)), F32)
    hi = lax.bitcast_convert_type(u & jnp.uint32(0xFFFF0000), F32)
    return lo, hi


def _store_packed_rows(ref, x, nrows):
    half = x.shape[1] // 2
    nslab = half // LANES
    for s in range(nslab):
        lo = x[:, s * LANES:(s + 1) * LANES]
        hi = x[:, half + s * LANES:half + (s + 1) * LANES]
        ref[pl.ds(s, nrows, stride=nslab), :] = _pack_bf16_pair(lo, hi)


def _mix_out_kernel(yr_ref, yf_ref, ym_ref, x_ref, eg_ref, eb_ref, woa_ref, wob_ref, woc_ref,
                    g_ref, b_ref, rw_ref, rb_ref, h_ref, hp_ref, ti_ref, tg_ref, *, alpha, n_experts):
    bm, d = x_ref.shape
    h0 = _layer_norm_rows(x_ref[...], eg_ref[...], eb_ref[...], LN_EPS)
    mixed = _dot(yr_ref[...], woa_ref[...]) + _dot(yf_ref[...], wob_ref[...]) + _dot(ym_ref[...], woc_ref[...])
    h1 = _layer_norm_rows(alpha * h0 + mixed, g_ref[...], b_ref[...], LN_EPS)
    h_ref[...] = h1
    _store_packed_rows(hp_ref, h1, bm)

    logits = _dot(h1, rw_ref[...], HIGHEST) + rb_ref[...]
    lane = lax.broadcasted_iota(jnp.int32, logits.shape, 1)
    work = jnp.where(lane < n_experts, logits, -jnp.inf)
    vals, idxs = [], []
    for _ in range(TOP_K):
        m = jnp.max(work, axis=-1, keepdims=True)
        idx = jnp.min(jnp.where(work == m, lane, LANES), axis=-1, keepdims=True)
        work = jnp.where(lane == idx, -jnp.inf, work)
        vals.append(m)
        idxs.append(idx)
    exps = [jnp.exp(v_ - vals[0]) for v_ in vals]
    denom = exps[0] + exps[1] + exps[2] + exps[3]
    ti = jnp.zeros(logits.shape, jnp.int32)
    tg = jnp.zeros(logits.shape, F32)
    for k_ in range(TOP_K):
        ti = jnp.where(lane == k_, idxs[k_], ti)
        tg = jnp.where(lane == k_, exps[k_] / denom, tg)
    ti_ref[...] = ti
    tg_ref[...] = tg


def mix_out(yr, yf, ym, x, emb_g, emb_b, wo, ln_g, ln_b, router_w, router_b, *, alpha, n_experts, bm):
    n, d = x.shape
    wr, wf, wm = yr.shape[1], yf.shape[1], ym.shape[1]
    assert wf == wm and wr % wf == 0 and d % (2 * LANES) == 0
    nslab = d // (2 * LANES)
    row = lambda w_: pl.BlockSpec((bm, w_), lambda i: (i, 0))
    const = lambda r_, w_, ri=0: pl.BlockSpec((r_, w_), lambda i: (ri, 0))
    return pl.pallas_call(
        functools.partial(_mix_out_kernel, alpha=alpha, n_experts=n_experts),
        grid=(n // bm,),
        in_specs=[row(wr), row(wf), row(wm), row(d), const(1, d), const(1, d),
                  const(wr, d), const(wf, d, wr // wf), const(wm, d, wr // wf + 1),
                  const(1, d), const(1, d), const(d, LANES), const(1, LANES)],
        out_specs=[row(d), pl.BlockSpec((bm * nslab, LANES), lambda i: (i, 0)), row(LANES), row(LANES)],
        out_shape=[jax.ShapeDtypeStruct((n, d), F32),
                   jax.ShapeDtypeStruct((n * nslab, LANES), jnp.uint32),
                   jax.ShapeDtypeStruct((n, LANES), jnp.int32),
                   jax.ShapeDtypeStruct((n, LANES), F32)],
        compiler_params=_cparams(("parallel",)),
        name="mix_out",
    )(yr, yf, ym, x, emb_g.reshape(1, d), emb_b.reshape(1, d), wo, wo, wo,
      ln_g.reshape(1, d), ln_b.reshape(1, d), router_w, router_b)


def _moe_kernel(be_ref, nu_ref, tok_ref, h_ref, w1g_ref, w1l_ref, b1g_ref, b1l_ref, w2_ref, b2_ref,
                o_ref, idx_ref, xraw_ref, xb_ref, acc_ref, idx_sem, row_sem, *, nslab):
    bm = xb_ref.shape[0]
    i = pl.program_id(0)
    f = pl.program_id(1)
    nf = pl.num_programs(1)
    n_used = nu_ref[0]

    def fetch_rows(blk):
        cp = pltpu.make_async_copy(tok_ref.at[blk], idx_ref, idx_sem)
        cp.start()
        cp.wait()

        def issue(r, carry):
            src = pl.multiple_of(idx_ref[r] * nslab, nslab)
            dst = pl.multiple_of(r * nslab, nslab)
            pltpu.make_async_copy(h_ref.at[pl.ds(src, nslab), :], xraw_ref.at[pl.ds(dst, nslab), :], row_sem).start()
            return carry

        lax.fori_loop(0, bm, issue, 0)

    @pl.when(i < n_used)
    def _():
        @pl.when(f == 0)
        def _():
            @pl.when(i == 0)
            def _():
                fetch_rows(0)

            pltpu.make_async_copy(h_ref.at[pl.ds(0, bm * nslab), :], xraw_ref, row_sem).wait()
            half = nslab * LANES
            for s in range(nslab):
                lo, hi = _unpack_bf16_pair(xraw_ref[pl.ds(s, bm, stride=nslab), :])
                xb_ref[:, s * LANES:(s + 1) * LANES] = lo.astype(BF16)
                xb_ref[:, half + s * LANES:half + (s + 1) * LANES] = hi.astype(BF16)

            @pl.when(i + 1 < n_used)
            def _():
                fetch_rows(i + 1)

        xb = xb_ref[...]
        hg = _dot(xb, w1g_ref[...].astype(BF16)) + b1g_ref[...]
        hl = _dot(xb, w1l_ref[...].astype(BF16)) + b1l_ref[...]
        glu = jnp.minimum(hg, SWIGLU_LIMIT)
        lin = jnp.clip(hl, -SWIGLU_LIMIT, SWIGLU_LIMIT)
        act = (lin + 1.0) * glu * _sigmoid(SWIGLU_ALPHA * glu)
        part = _dot(act.astype(BF16), w2_ref[...].astype(BF16))

        @pl.when(f == 0)
        def _():
            acc_ref[...] = part

        @pl.when(f > 0)
        def _():
            acc_ref[...] += part

        @pl.when(f == nf - 1)
        def _():
            _store_packed_rows(o_ref, acc_ref[...] + b2_ref[...], bm)

    @pl.when((i >= n_used) & (f == nf - 1))
    def _():
        o_ref[...] = jnp.zeros_like(o_ref)


def moe_experts(block_e, n_used, tok_blocks, h_rows, w1, b1, w2, b2, *, bm, tf):
    nblk = tok_blocks.shape[0]
    n_exp, d, de2 = w1.shape
    de = de2 // 2
    nslab = d // (2 * LANES)
    nf = de // tf

    def blk(i, nu):
        return jnp.minimum(i, nu[0] - 1)

    def fi(i, f, nu):
        return jnp.where(i < nu[0], f, nf - 1)

    grid_spec = pltpu.PrefetchScalarGridSpec(
        num_scalar_prefetch=2,
        grid=(nblk, nf),
        in_specs=[
            pl.BlockSpec(memory_space=pl.ANY),
            pl.BlockSpec(memory_space=pl.ANY),
            pl.BlockSpec((None, d, tf), lambda i, f, be, nu: (be[blk(i, nu)], 0, fi(i, f, nu))),
            pl.BlockSpec((None, d, tf), lambda i, f, be, nu: (be[blk(i, nu)], 0, nf + fi(i, f, nu))),
            pl.BlockSpec((None, 1, tf), lambda i, f, be, nu: (be[blk(i, nu)], 0, fi(i, f, nu))),
            pl.BlockSpec((None, 1, tf), lambda i, f, be, nu: (be[blk(i, nu)], 0, nf + fi(i, f, nu))),
            pl.BlockSpec((None, tf, d), lambda i, f, be, nu: (be[blk(i, nu)], fi(i, f, nu), 0)),
            pl.BlockSpec((None, 1, d), lambda i, f, be, nu: (be[blk(i, nu)], 0, 0)),
        ],
        out_specs=pl.BlockSpec((bm * nslab, LANES), lambda i, f, be, nu: (i, 0)),
        scratch_shapes=[
            pltpu.SMEM((bm,), jnp.int32),
            pltpu.VMEM((bm * nslab, LANES), jnp.uint32),
            pltpu.VMEM((bm, d), BF16),
            pltpu.VMEM((bm, d), F32),
            pltpu.SemaphoreType.DMA,
            pltpu.SemaphoreType.DMA,
        ],
    )
    return pl.pallas_call(
        functools.partial(_moe_kernel, nslab=nslab),
        grid_spec=grid_spec,
        out_shape=jax.ShapeDtypeStruct((nblk * bm * nslab, LANES), jnp.uint32),
        compiler_params=_cparams(("arbitrary", "arbitrary")),
        name="moe_experts",
    )(block_e, n_used, tok_blocks, h_rows, w1, w1, b1, b1, w2, b2)


def _combine_kernel(pos_ref, y_ref, h_ref, tg_ref, g_ref, b_ref, o_ref, idx_ref, ybuf_ref, idx_sem, row_sem,
                    *, alpha, nslab):
    tm = o_ref.shape[0]
    nrow = TOP_K * tm
    i = pl.program_id(0)
    n_steps = pl.num_programs(0)

    def fetch_rows(step, slot):
        cp = pltpu.make_async_copy(pos_ref.at[step], idx_ref.at[slot], idx_sem)
        cp.start()
        cp.wait()

        def issue(r, carry):
            src = pl.multiple_of(idx_ref[slot, r] * nslab, nslab)
            dst = pl.multiple_of(r * nslab, nslab)
            pltpu.make_async_copy(y_ref.at[pl.ds(src, nslab), :], ybuf_ref.at[slot, pl.ds(dst, nslab), :],
                                  row_sem.at[slot]).start()
            return carry

        lax.fori_loop(0, nrow, issue, 0)

    slot = i % 2

    @pl.when(i == 0)
    def _():
        fetch_rows(0, 0)

    @pl.when(i + 1 < n_steps)
    def _():
        fetch_rows(i + 1, 1 - slot)

    pltpu.make_async_copy(y_ref.at[pl.ds(0, nrow * nslab), :], ybuf_ref.at[slot], row_sem.at[slot]).wait()

    gates = [tg_ref[:, k_:k_ + 1] for k_ in range(TOP_K)]
    lo_slabs, hi_slabs = [], []
    for s in range(nslab):
        acc_lo = acc_hi = None
        for k_ in range(TOP_K):
            lo, hi = _unpack_bf16_pair(ybuf_ref[slot, pl.ds(k_ * tm * nslab + s, tm, stride=nslab), :])
            acc_lo = gates[k_] * lo if acc_lo is None else acc_lo + gates[k_] * lo
            acc_hi = gates[k_] * hi if acc_hi is None else acc_hi + gates[k_] * hi
        lo_slabs.append(acc_lo)
        hi_slabs.append(acc_hi)
    pre = alpha * h_ref[...] + jnp.concatenate(lo_slabs + hi_slabs, axis=-1)
    o_ref[...] = _layer_norm_rows(pre, g_ref[...], b_ref[...], LN_EPS).astype(o_ref.dtype)


def moe_combine(pos_blocks, y_rows, h1, gates, ln_g, ln_b, *, alpha, tm, out_dtype):
    n = gates.shape[0]
    d = ln_g.shape[-1]
    nslab = d // (2 * LANES)
    nrow = TOP_K * tm
    return pl.pallas_call(
        functools.partial(_combine_kernel, alpha=alpha, nslab=nslab),
        grid=(n // tm,),
        in_specs=[pl.BlockSpec(memory_space=pl.ANY),
                  pl.BlockSpec(memory_space=pl.ANY),
                  pl.BlockSpec((tm, d), lambda i: (i, 0)),
                  pl.BlockSpec((tm, LANES), lambda i: (i, 0)),
                  pl.BlockSpec((1, d), lambda i: (0, 0)),
                  pl.BlockSpec((1, d), lambda i: (0, 0))],
        out_specs=pl.BlockSpec((tm, d), lambda i: (i, 0)),
        out_shape=jax.ShapeDtypeStruct((n, d), out_dtype),
        scratch_shapes=[pltpu.SMEM((2, nrow), jnp.int32),
                        pltpu.VMEM((2, nrow * nslab, LANES), jnp.uint32),
                        pltpu.SemaphoreType.DMA,
                        pltpu.SemaphoreType.DMA((2,))],
        compiler_params=_cparams(("arbitrary",)),
        name="moe_combine",
    )(pos_blocks, y_rows, h1, gates, ln_g.reshape(1, d), ln_b.reshape(1, d))


def _route(top_idx, n_experts, bm, tm):
    n = top_idx.shape[0]
    n_assign = n * TOP_K
    e = top_idx.reshape(-1)
    onehot = (e[:, None] == jnp.arange(n_experts, dtype=jnp.int32)[None, :]).astype(jnp.int32)
    csum = jnp.cumsum(onehot, axis=0)
    rank = jnp.sum((csum - onehot) * onehot, axis=-1)
    counts = csum[-1]
    padded = ((counts + bm - 1) // bm) * bm
    pend = jnp.cumsum(padded)
    pstart = pend - padded
    pos = jnp.sum(onehot * pstart[None, :], axis=-1) + rank
    nblk = n_assign // bm + n_experts
    tok = jnp.arange(n_assign, dtype=jnp.int32) // TOP_K
    buf_tok = jnp.zeros((nblk * bm,), jnp.int32).at[pos].set(tok)
    block_e = jnp.clip(jnp.searchsorted(pend, jnp.arange(nblk, dtype=jnp.int32) * bm, side="right"),
                       0, n_experts - 1).astype(jnp.int32)
    n_used = (pend[-1] // bm).astype(jnp.int32).reshape(1)
    pos_blocks = pos.reshape(n // tm, tm, TOP_K).transpose(0, 2, 1).reshape(n // tm, TOP_K * tm)
    return block_e, n_used, buf_tok.reshape(nblk, bm), pos_blocks.astype(jnp.int32)


RWKV_HEADS = 16
DECAY_LORA = 96
ICLR_LORA = 96
GATE_LORA = 256
FOX_HEADS = 8
MEM_HEADS = 4
MEM_HEAD_DIM = 128
N_EXPERTS = 32
LORA_PAD = 128

TILES = dict(proj_bm=1024, proj_bn=896, qkvm_bn=1024, rwkv_tt=1024, fox_tt=512, fox_bq=1024, mem_tq=1024,
             mix_bm=512, moe_bm=1024, moe_tf=256, comb_tm=256)


def _pad_cols(a, width):
    return jnp.pad(a, [(0, 0)] * (a.ndim - 1) + [(0, width - a.shape[-1])])


def _layer(x, mem_n_kv, emb_ln_g, emb_ln_b, w_in, mu, w0, wd, a0, wa, wg, k_k, k_a, r_k, gn_g, gn_b, b_f,
           w_o, ln1_g, ln1_b, router_w, router_b, w1, b1, w2, b2, ln2_g, ln2_b, alpha, tiles):
    bsz, t_len, d = x.shape
    n = bsz * t_len
    wr = RWKV_HEADS * HEAD64
    wf = FOX_HEADS * HEAD64
    wm = MEM_HEADS * MEM_HEAD_DIM
    c0 = 3 * wr
    c1 = c0 + DECAY_LORA
    c2 = c1 + ICLR_LORA
    c3 = c2 + GATE_LORA
    f0 = c3 + 3 * wf
    f1 = f0 + FOX_HEADS
    tl = lambda name, dim: min(tiles[name], dim)

    def rwkv_cols(a):
        return jnp.concatenate([a[..., :c0], _pad_cols(a[..., c0:c1], LORA_PAD),
                                _pad_cols(a[..., c1:c2], LORA_PAD), a[..., c2:c3]], axis=-1)

    w_rwkv = rwkv_cols(w_in).astype(BF16)
    w_qkvm = jnp.concatenate([w_in[:, c3:f0], w_in[:, f1:]], axis=-1).astype(BF16)
    w_f = _pad_cols(w_in[:, f0:f1], LANES).astype(BF16)
    x2 = x.reshape(n, d)
    bm = tl("proj_bm", n)
    p_rwkv = ln_matmul(x2, emb_ln_g, emb_ln_b, w_rwkv, F32, bm, tl("proj_bn", w_rwkv.shape[1]))
    p_qkvm = ln_matmul(x2, emb_ln_g, emb_ln_b, w_qkvm, BF16, bm, tl("qkvm_bn", w_qkvm.shape[1]))
    f_log = ln_matmul(x2, emb_ln_g, emb_ln_b, w_f, F32, bm, LANES)

    zrows = lambda a, rows: jnp.pad(a, ((0, rows - a.shape[0]), (0, 0)))
    y_rwkv = rwkv_mix(p_rwkv.reshape(bsz, t_len, -1), rwkv_cols(mu), w0, a0, k_k, k_a, r_k.reshape(-1), gn_g, gn_b,
                      zrows(wd, LORA_PAD), zrows(wa, LORA_PAD), wg,
                      width=wr, tt=tl("rwkv_tt", t_len), out_dtype=BF16)

    c = fox_cumsum(f_log.reshape(bsz, t_len, LANES), _pad_cols(b_f.reshape(1, -1), LANES), tl("fox_tt", t_len))
    c_rows = c[..., :FOX_HEADS].transpose(0, 2, 1).reshape(bsz * FOX_HEADS, 1, t_len)
    qkv = p_qkvm[:, :3 * wf].reshape(bsz, t_len, 3, FOX_HEADS, HEAD64)
    qkv = qkv.transpose(2, 0, 3, 1, 4).reshape(3, bsz * FOX_HEADS, t_len, HEAD64)
    o_fox = fox_attention(qkv[0], qkv[1].transpose(0, 2, 1), qkv[2], c_rows, bq=tl("fox_bq", t_len), out_dtype=BF16)
    y_fox = o_fox.reshape(bsz, FOX_HEADS, t_len, HEAD64).transpose(0, 2, 1, 3).reshape(n, wf)

    m_len = mem_n_kv.shape[0] // bsz
    km = mem_n_kv[:, :wm].reshape(bsz, m_len, MEM_HEADS, MEM_HEAD_DIM).transpose(0, 2, 3, 1).astype(BF16)
    vm = mem_n_kv[:, wm:].reshape(bsz, m_len, MEM_HEADS, MEM_HEAD_DIM).transpose(0, 2, 1, 3).astype(BF16)
    y_mem = mem_attention(p_qkvm.reshape(bsz, t_len, -1), (3 * wf) // wm, km, vm, heads=MEM_HEADS, width=wm,
                          tq=tl("mem_tq", t_len), out_dtype=BF16)

    rw = _pad_cols(router_w, LANES)
    rb = _pad_cols(router_b.reshape(1, -1), LANES)
    h1, h_rows, top_i, top_g = mix_out(y_rwkv.reshape(n, wr), y_fox, y_mem.reshape(n, wm), x2, emb_ln_g, emb_ln_b,
                                   w_o.astype(BF16), ln1_g, ln1_b, rw, rb, alpha=alpha, n_experts=N_EXPERTS,
                                   bm=tl("mix_bm", n))

    moe_bm = tl("moe_bm", n)
    tm = tl("comb_tm", n)
    block_e, n_used, tok_blocks, pos_blocks = _route(top_i[:, :TOP_K], N_EXPERTS, moe_bm, tm)
    y_rows = moe_experts(block_e, n_used, tok_blocks, h_rows, w1, b1[:, None, :], w2, b2[:, None, :],
                         bm=moe_bm, tf=tl("moe_tf", w2.shape[1]))
    out = moe_combine(pos_blocks, y_rows, h1, top_g, ln2_g, ln2_b, alpha=alpha, tm=tm, out_dtype=x.dtype)
    return out.reshape(bsz, t_len, d)


def kernel(x, mem, emb_ln_g, emb_ln_b, w_in, rwkv_mu, rwkv_w0, rwkv_w_decay_up, rwkv_a0, rwkv_w_iclr_up,
           rwkv_w_gate_up, rwkv_k_k, rwkv_k_a, rwkv_r_k, rwkv_gn_g, rwkv_gn_b, fox_b_f, mem_ln_g, mem_ln_b,
           w_mem_kv, w_o, ln1_g, ln1_b, router_w, router_b, expert_w1, expert_b1, expert_w2, expert_b2,
           ln2_g, ln2_b):
    depth = w_in.shape[0]
    assert depth == 1
    alpha = (2 * depth) ** 0.25
    bsz, m_len, d = mem.shape
    kv = ln_matmul(mem.reshape(bsz * m_len, d), mem_ln_g, mem_ln_b, w_mem_kv[0].astype(BF16), F32,
                   min(512, bsz * m_len), w_mem_kv.shape[-1])
    return _layer(x, kv, emb_ln_g, emb_ln_b, w_in[0], rwkv_mu[0], rwkv_w0[0], rwkv_w_decay_up[0], rwkv_a0[0],
                  rwkv_w_iclr_up[0], rwkv_w_gate_up[0], rwkv_k_k[0], rwkv_k_a[0], rwkv_r_k[0], rwkv_gn_g[0],
                  rwkv_gn_b[0], fox_b_f[0], w_o[0], ln1_g[0], ln1_b[0], router_w[0], router_b[0],
                  expert_w1[0], expert_b1[0], expert_w2[0], expert_b2[0], ln2_g[0], ln2_b[0], alpha, TILES)
```

```python
import functools
import math

import jax
import jax.numpy as jnp
from jax import lax
from jax.experimental import pallas as pl
from jax.experimental.pallas import tpu as pltpu

F32 = jnp.float32
BF16 = jnp.bfloat16
HIGHEST = lax.Precision.HIGHEST

LANES = 128
HEAD64 = 64
VMEM_LIMIT = 56 * 1024 * 1024

LN_EPS = 1e-5
RWKV_GN_EPS = 64e-5
SWIGLU_LIMIT = 7.0
SWIGLU_ALPHA = 1.702
TOP_K = 4
LOG2E = 1.4426950408889634


def _cparams(sem):
    return pltpu.CompilerParams(dimension_semantics=sem, vmem_limit_bytes=VMEM_LIMIT)


def _dot(a, b, precision=None):
    return jnp.dot(a, b, preferred_element_type=F32, precision=precision)


def _dot_nt(a, b, precision=None):
    return lax.dot_general(a, b, (((1,), (1,)), ((), ())), preferred_element_type=F32, precision=precision)


def _dot_tn(a, b, precision=None):
    return lax.dot_general(a, b, (((0,), (0,)), ((), ())), preferred_element_type=F32, precision=precision)


def _dot_split(a, b, split_rhs=False):
    x = b if split_rhs else a
    hi = x.astype(BF16)
    lo = (x - hi.astype(F32)).astype(BF16)
    if split_rhs:
        return _dot(a, hi) + _dot(a, lo)
    return _dot(hi, b) + _dot(lo, b)


def _layer_norm_rows(x, g, b, eps):
    mean = jnp.mean(x, axis=-1, keepdims=True)
    xc = x - mean
    var = jnp.mean(xc * xc, axis=-1, keepdims=True)
    return xc * lax.rsqrt(var + eps) * g + b


def _sigmoid(x):
    return 1.0 / (1.0 + jnp.exp(-x))


def _ln_matmul_kernel(x_ref, g_ref, b_ref, w_ref, o_ref, xn_ref):
    @pl.when(pl.program_id(1) == 0)
    def _():
        xn_ref[...] = _layer_norm_rows(x_ref[...], g_ref[...], b_ref[...], LN_EPS).astype(BF16)

    o_ref[...] = _dot(xn_ref[...], w_ref[...]).astype(o_ref.dtype)


def ln_matmul(x, g, b, w, out_dtype, bm, bn):
    n, d = x.shape
    m = w.shape[1]
    assert n % bm == 0 and m % bn == 0
    return pl.pallas_call(
        _ln_matmul_kernel,
        grid=(n // bm, m // bn),
        in_specs=[
            pl.BlockSpec((bm, d), lambda i, j: (i, 0)),
            pl.BlockSpec((1, d), lambda i, j: (0, 0)),
            pl.BlockSpec((1, d), lambda i, j: (0, 0)),
            pl.BlockSpec((d, bn), lambda i, j: (0, j)),
        ],
        out_specs=pl.BlockSpec((bm, bn), lambda i, j: (i, j)),
        out_shape=jax.ShapeDtypeStruct((n, m), out_dtype),
        scratch_shapes=[pltpu.VMEM((bm, d), BF16)],
        compiler_params=_cparams(("parallel", "arbitrary")),
        name="ln_matmul",
    )(x, g.reshape(1, d), b.reshape(1, d), w)


RWKV_CHUNK = 64


def _rwkv_kernel(r_ref, k_ref, v_ref, dw_ref, da_ref, dg_ref,
                 mur_ref, muk_ref, muv_ref, mudw_ref, muda_ref, mudg_ref,
                 w0_ref, a0_ref, kk_ref, ka_ref, rk_ref, gng_ref, gnb_ref,
                 wd_ref, wa_ref, wg_ref,
                 o_ref,
                 cr_ref, ck_ref, cv_ref, cdw_ref, cda_ref, cdg_ref, h_ref,
                 sr_ref, slw_ref, sk_ref, sv_ref, sa_ref, sb_ref, sy_ref,
                 rp_ref, y0_ref, m_ref, ha_ref):
    tt, gw = r_ref.shape
    nh = gw // HEAD64
    C = RWKV_CHUNK
    S = nh * C

    @pl.when(pl.program_id(2) == 0)
    def _():
        for c in (cr_ref, ck_ref, cv_ref, cdw_ref, cda_ref, cdg_ref, h_ref):
            c[...] = jnp.zeros_like(c)

    def shift(x_ref, mu_ref, carry_ref):
        x = x_ref[...]
        rows = lax.broadcasted_iota(jnp.int32, x.shape, 0)
        prev = jnp.where(rows == 0, carry_ref[0:1, :], pltpu.roll(x, 1, axis=0))
        carry_ref[0:1, :] = x[tt - 1:tt, :]
        return x + mu_ref[...] * (prev - x)

    r = shift(r_ref, mur_ref, cr_ref)
    k = shift(k_ref, muk_ref, ck_ref)
    v = shift(v_ref, muv_ref, cv_ref)
    dw = shift(dw_ref, mudw_ref, cdw_ref)
    da = shift(da_ref, muda_ref, cda_ref)
    dg = shift(dg_ref, mudg_ref, cdg_ref)

    lane = lax.broadcasted_iota(jnp.int32, (gw, gw), 1)
    row = lax.broadcasted_iota(jnp.int32, (gw, gw), 0)
    seg_ones = jnp.where((lane // HEAD64) == (row // HEAD64), 1.0, 0.0).astype(BF16)

    def segsum(x):
        return _dot_split(x, seg_ones)

    bdot = lambda a_, b_: _dot(a_.astype(BF16), b_.astype(BF16))
    z = w0_ref[...] + bdot(jnp.tanh(dw), wd_ref[...])
    softplus_neg = jnp.maximum(-z, 0.0) + jnp.log(1.0 + jnp.exp(-jnp.abs(z)))
    lw = -jnp.exp(-softplus_neg - 0.5)
    a = _sigmoid(a0_ref[...] + bdot(da, wa_ref[...]))
    gate = bdot(_sigmoid(dg), wg_ref[...])
    kk = k * kk_ref[...]
    kk = kk / jnp.maximum(jnp.sqrt(segsum(kk * kk)), 1e-12)
    k2 = k * (1.0 + (a - 1.0) * ka_ref[...])
    bonus = segsum(r * k2 * rk_ref[...]) * v

    sr_ref[...] = r
    slw_ref[...] = lw
    sk_ref[...] = k2
    sv_ref[...] = v
    sa_ref[...] = -kk
    sb_ref[...] = kk * a

    head_of_lane = lax.broadcasted_iota(jnp.int32, (C, gw), 1) // HEAD64
    rr = lax.broadcasted_iota(jnp.int32, (S, S), 0)
    cc = lax.broadcasted_iota(jnp.int32, (S, S), 1)
    same_head = (rr // C) == (cc // C)
    strict = same_head & ((rr % C) > (cc % C))
    incl = same_head & ((rr % C) >= (cc % C))
    eye_s = jnp.where(rr == cc, 1.0, 0.0).astype(F32)
    rc = lax.broadcasted_iota(jnp.int32, (C, C), 0)
    cc1 = lax.broadcasted_iota(jnp.int32, (C, C), 1)
    tri_incl = jnp.where(rc >= cc1, 1.0, 0.0).astype(BF16)
    diag_mask = row == lane

    def stack(x):
        return jnp.concatenate([jnp.where(head_of_lane == h_, x, 0.0) for h_ in range(nh)], axis=0)

    def chunk_terms(g, carry):
        cs = [g * group + i for i in range(group)]
        each = lambda fn, *lists: [fn(*args) for args in zip(*lists)]
        sls = [pl.ds(pl.multiple_of(c * C, C), C) for c in cs]
        sqs = [pl.ds(pl.multiple_of(c * S, S), S) for c in cs]
        sgs = [pl.ds(pl.multiple_of(c * gw, gw), gw) for c in cs]
        lws = [slw_ref[sl, :] for sl in sls]
        lgs = each(lambda lw_: _dot_split(tri_incl, lw_, split_rhs=True), lws)
        lasts = [lg[C - 1:C, :] for lg in lgs]
        e_negs = [jnp.exp(-lg) for lg in lgs]
        e_ends = each(lambda lg, last: jnp.exp(last - lg), lgs, lasts)
        rts = each(lambda sl, lg: stack(sr_ref[sl, :] * jnp.exp(lg)), sls, lgs)
        ats = each(lambda sl, lg, lw_: stack(sa_ref[sl, :] * jnp.exp(lg - lw_)).astype(BF16), sls, lgs, lws)
        bts = each(lambda sl, e: stack(sb_ref[sl, :] * e).astype(BF16), sls, e_negs)
        kts = each(lambda sl, e: stack(sk_ref[sl, :] * e).astype(BF16), sls, e_negs)
        bhs = each(lambda sl, e: stack(sb_ref[sl, :] * e).astype(BF16), sls, e_ends)
        khs = each(lambda sl, e: stack(sk_ref[sl, :] * e).astype(BF16), sls, e_ends)
        vss = [stack(sv_ref[sl, :]).astype(BF16) for sl in sls]

        quads = each(lambda at, rt, bt, kt: _dot_nt(jnp.concatenate([at, rt.astype(BF16)], axis=0),
                                                    jnp.concatenate([bt, kt], axis=0)), ats, rts, bts, kts)
        a_abs = [jnp.where(strict, q_[:S, :S], 0.0) for q_ in quads]
        a_aks = [jnp.where(strict, q_[:S, S:], 0.0).astype(BF16) for q_ in quads]
        a_rbs = [jnp.where(incl, q_[S:, :S], 0.0).astype(BF16) for q_ in quads]
        a_rks = [jnp.where(incl, q_[S:, S:], 0.0).astype(BF16) for q_ in quads]

        pws = [_dot(n_.astype(BF16), n_.astype(BF16)) for n_ in a_abs]
        invs = [eye_s + n_ for n_ in a_abs]
        n_steps = int(math.log2(C)) - 1
        for step in range(n_steps):
            pwbs = [pw.astype(BF16) for pw in pws]
            if step + 1 < n_steps:
                boths = each(lambda pwb, inv: _dot(jnp.concatenate([pwb, inv.astype(BF16)], axis=0), pwb), pwbs, invs)
                pws = [b_[:S, :] for b_ in boths]
                invs = each(lambda inv, b_: inv + b_[S:, :], invs, boths)
            else:
                invs = each(lambda inv, pwb: inv + _dot(inv.astype(BF16), pwb), invs, pwbs)
        invbs = [inv.astype(BF16) for inv in invs]

        akvs = each(lambda a_ak, vs: _dot(a_ak, vs).astype(BF16), a_aks, vss)
        trs = each(lambda invb, at, akv: _dot(invb, jnp.concatenate([at, akv], axis=1)), invbs, ats, akvs)
        atps = [tr[:, :gw].astype(BF16) for tr in trs]
        uvs = each(lambda tr, vs: jnp.concatenate([tr[:, gw:].astype(BF16), vs], axis=0), trs, vss)
        for i in range(group):
            rp_ref[sqs[i], :] = (rts[i] + _dot(a_rbs[i], atps[i])).astype(BF16)
            y0_ref[sqs[i], :] = _dot(jnp.concatenate([a_rbs[i], a_rks[i]], axis=1), uvs[i])
            m_ref[sgs[i], :] = (jnp.where(diag_mask, jnp.exp(lasts[i]), 0.0) + _dot_tn(bhs[i], atps[i])).astype(BF16)
            ha_ref[sgs[i], :] = _dot_tn(jnp.concatenate([bhs[i], khs[i]], axis=0), uvs[i])
        return carry

    group = math.gcd(tt // C, 4)
    lax.fori_loop(0, tt // (C * group), chunk_terms, 0)

    def chunk_state(c, carry):
        sl = pl.ds(pl.multiple_of(c * C, C), C)
        sq = pl.ds(pl.multiple_of(c * S, S), S)
        sg = pl.ds(pl.multiple_of(c * gw, gw), gw)
        h0 = h_ref[...]
        y = y0_ref[sq, :] + _dot_split(rp_ref[sq, :], h0, split_rhs=True)
        y_sum = y[0:C, :]
        for h_ in range(1, nh):
            y_sum = y_sum + y[h_ * C:(h_ + 1) * C, :]
        sy_ref[sl, :] = y_sum
        h_ref[...] = ha_ref[sg, :] + _dot_split(m_ref[sg, :], h0, split_rhs=True)
        return carry

    lax.fori_loop(0, tt // C, chunk_state, 0)

    y = sy_ref[...]
    mean = segsum(y) * (1.0 / HEAD64)
    yc = y - mean
    var = segsum(yc * yc) * (1.0 / HEAD64)
    yn = yc * lax.rsqrt(var + RWKV_GN_EPS) * gng_ref[...] + gnb_ref[...]
    o_ref[...] = ((yn + bonus) * gate).astype(o_ref.dtype)


def rwkv_mix(p, mu, w0, a0, k_k, k_a, r_k, gn_g, gn_b, wd, wa, wg, *, width, tt, gw, out_dtype):
    bsz, t_len, _ = p.shape
    assert gw % LANES == 0 and width % gw == 0 and (3 * width) % 256 == 0
    assert t_len % tt == 0 and tt % RWKV_CHUNK == 0
    ngrp = width // gw
    col_dw = 3 * width // LANES
    col_da = col_dw + 1
    col_dg = (3 * width + 256) // 256
    nchunk = tt // RWKV_CHUNK
    stacked_rows = (gw // HEAD64) * RWKV_CHUNK

    def tok(width_, col):
        return pl.BlockSpec((None, tt, width_), lambda b, h, t: (b, t, col(h)))

    def par(rows, width_, col):
        return pl.BlockSpec((rows, width_), lambda b, h, t: (0, col(h)))

    vec = lambda a_: a_.reshape(1, -1)
    cols = [lambda h: h, lambda h: ngrp + h, lambda h: 2 * ngrp + h,
            lambda h: col_dw, lambda h: col_da, lambda h: col_dg]
    widths = [gw, gw, gw, LANES, LANES, 256]
    in_specs = [tok(w_, c_) for w_, c_ in zip(widths, cols)]
    in_specs += [par(1, w_, c_) for w_, c_ in zip(widths, cols)]
    in_specs += [par(1, gw, lambda h: h)] * 7
    in_specs += [par(LANES, gw, lambda h: h), par(LANES, gw, lambda h: h), par(256, gw, lambda h: h)]
    mu2 = vec(mu)
    args = [p] * 6 + [mu2] * 6 + [vec(w0), vec(a0), vec(k_k), vec(k_a), vec(r_k), vec(gn_g), vec(gn_b), wd, wa, wg]
    carry = lambda w_: pltpu.VMEM((8, w_), F32)
    tile = pltpu.VMEM((tt, gw), F32)
    return pl.pallas_call(
        _rwkv_kernel,
        grid=(bsz, ngrp, t_len // tt),
        in_specs=in_specs,
        out_specs=pl.BlockSpec((None, tt, gw), lambda b, h, t: (b, t, h)),
        out_shape=jax.ShapeDtypeStruct((bsz, t_len, width), out_dtype),
        scratch_shapes=([carry(gw)] * 3 + [carry(LANES)] * 2 + [carry(256), pltpu.VMEM((gw, gw), F32)]
                        + [tile] * 7
                        + [pltpu.VMEM((nchunk * stacked_rows, gw), BF16), pltpu.VMEM((nchunk * stacked_rows, gw), F32)]
                        + [pltpu.VMEM((nchunk * gw, gw), BF16), pltpu.VMEM((nchunk * gw, gw), F32)]),
        compiler_params=_cparams(("parallel", "parallel", "arbitrary")),
        name="rwkv_mix",
    )(*args)


def _fox_cumsum_kernel(f_ref, bf_ref, o_ref, carry_ref):
    @pl.when(pl.program_id(1) == 0)
    def _():
        carry_ref[...] = jnp.zeros_like(carry_ref)

    tt = f_ref.shape[0]
    z = f_ref[...] + bf_ref[...]
    log_f = -LOG2E * (jnp.maximum(-z, 0.0) + jnp.log(1.0 + jnp.exp(-jnp.abs(z))))
    rr = lax.broadcasted_iota(jnp.int32, (tt, tt), 0)
    cc = lax.broadcasted_iota(jnp.int32, (tt, tt), 1)
    tri = jnp.where(rr >= cc, 1.0, 0.0).astype(F32)
    c = _dot(tri, log_f, HIGHEST) + carry_ref[0:1, :]
    o_ref[...] = c
    carry_ref[0:1, :] = c[tt - 1:tt, :]


def fox_cumsum(f_logits, b_f, tt):
    bsz, t_len, _ = f_logits.shape
    return pl.pallas_call(
        _fox_cumsum_kernel,
        grid=(bsz, t_len // tt),
        in_specs=[pl.BlockSpec((None, tt, LANES), lambda b, t: (b, t, 0)),
                  pl.BlockSpec((1, LANES), lambda b, t: (0, 0))],
        out_specs=pl.BlockSpec((None, tt, LANES), lambda b, t: (b, t, 0)),
        out_shape=jax.ShapeDtypeStruct((bsz, t_len, LANES), F32),
        scratch_shapes=[pltpu.VMEM((8, LANES), F32)],
        compiler_params=_cparams(("parallel", "arbitrary")),
        name="fox_cumsum",
    )(f_logits, b_f)


def _fox_attn_kernel(q_ref, kt_ref, v_ref, c_ref, o_ref, m_ref, acc_ref, *, sub, dh):
    bq = q_ref.shape[0]
    nsub = bq // sub
    qi = pl.program_id(1)
    q_start = pl.multiple_of(qi * bq, bq)
    c_q0 = c_ref[:, pl.ds(q_start, LANES)][:, 0:1]
    m_ref[...] = jnp.full_like(m_ref, -jnp.inf)
    acc_ref[...] = jnp.zeros_like(acc_ref)

    def block(k_start, widths, shift, masked):
        scores = [_dot(q_ref[r * sub:(r + 1) * sub, :], kt_ref[:, pl.ds(k_start, widths[r])]) for r in range(nsub)]
        probs, corrs = [], []
        for r in range(nsub):
            rows = slice(r * sub, (r + 1) * sub)
            s = scores[r]
            if masked:
                rr = lax.broadcasted_iota(jnp.int32, s.shape, 0) + r * sub
                cc = lax.broadcasted_iota(jnp.int32, s.shape, 1)
                s = jnp.where(cc <= rr, s, -jnp.inf)
            m_old = m_ref[rows, :]
            m_new = jnp.maximum(m_old, jnp.max(s, axis=-1, keepdims=True) + shift)
            probs.append(jnp.exp2(s - (m_new - shift)).astype(BF16))
            corrs.append(jnp.exp2(m_old - m_new))
            m_ref[rows, :] = m_new
        for r in range(nsub):
            rows = slice(r * sub, (r + 1) * sub)
            acc_ref[rows, :] = acc_ref[rows, :] * corrs[r] + _dot(probs[r], v_ref[pl.ds(k_start, widths[r]), :])

    def body(j, carry):
        k_start = pl.multiple_of(j * bq, bq)
        shift = c_q0 - c_ref[:, pl.ds(k_start, LANES)][:, 0:1]
        block(k_start, [bq] * nsub, shift, False)
        return carry

    lax.fori_loop(0, qi, body, 0)
    block(q_start, [(r + 1) * sub for r in range(nsub)], 0.0, True)
    acc = acc_ref[...]
    o_ref[...] = (acc[:, :dh] / acc[:, dh:dh + 1]).astype(o_ref.dtype)


def fox_attention(q_aug, kt_aug, v_aug, c2, *, bq, sub, dh, out_dtype):
    bh, t_len, _ = q_aug.shape
    return pl.pallas_call(
        functools.partial(_fox_attn_kernel, sub=sub, dh=dh),
        grid=(bh, t_len // bq),
        in_specs=[pl.BlockSpec((None, bq, LANES), lambda b, i: (b, i, 0)),
                  pl.BlockSpec((None, LANES, t_len), lambda b, i: (b, 0, 0)),
                  pl.BlockSpec((None, t_len, LANES), lambda b, i: (b, 0, 0)),
                  pl.BlockSpec((None, 1, t_len), lambda b, i: (b, 0, 0))],
        out_specs=pl.BlockSpec((None, bq, dh), lambda b, i: (b, i, 0)),
        out_shape=jax.ShapeDtypeStruct((bh, t_len, dh), out_dtype),
        scratch_shapes=[pltpu.VMEM((bq, 1), F32), pltpu.VMEM((bq, LANES), F32)],
        compiler_params=_cparams(("parallel", "arbitrary")),
        name="fox_attention",
    )(q_aug, kt_aug, v_aug, c2)


def _mem_attn_kernel(q_ref, kt_ref, v_ref, o_ref, *, heads, scale):
    dh = q_ref.shape[1] // heads
    outs = []
    for h in range(heads):
        q = q_ref[:, h * dh:(h + 1) * dh]
        s = _dot(q, kt_ref[h]) * scale
        m = jnp.max(s, axis=-1, keepdims=True)
        p = jnp.exp(s - m)
        o = _dot(p.astype(BF16), v_ref[h]) / jnp.sum(p, axis=-1, keepdims=True)
        outs.append(o)
    o_ref[...] = jnp.concatenate(outs, axis=-1).astype(o_ref.dtype)


def mem_attention(qsrc, q_col, kt, v, *, heads, width, tq, out_dtype):
    bsz, t_len, _ = qsrc.shape
    dh = width // heads
    m_len = v.shape[2]
    return pl.pallas_call(
        functools.partial(_mem_attn_kernel, heads=heads, scale=dh ** -0.5),
        grid=(bsz, t_len // tq),
        in_specs=[pl.BlockSpec((None, tq, width), lambda b, i: (b, i, q_col)),
                  pl.BlockSpec((None, heads, dh, m_len), lambda b, i: (b, 0, 0, 0)),
                  pl.BlockSpec((None, heads, m_len, dh), lambda b, i: (b, 0, 0, 0))],
        out_specs=pl.BlockSpec((None, tq, width), lambda b, i: (b, i, 0)),
        out_shape=jax.ShapeDtypeStruct((bsz, t_len, width), out_dtype),
        compiler_params=_cparams(("parallel", "parallel")),
        name="mem_attention",
    )(qsrc, kt, v)


def _pack_bf16_pair(lo, hi):
    lo_bits = lax.bitcast_convert_type(lo.astype(BF16).astype(F32), jnp.uint32)
    hi_bits = lax.bitcast_convert_type(hi.astype(BF16).astype(F32), jnp.uint32)
    return lax.shift_right_logical(lo_bits, jnp.uint32(16)) | (hi_bits & jnp.uint32(0xFFFF0000))


def _unpack_bf16_pair(u):
    lo = lax.bitcast_convert_type(lax.shift_left(u, jnp.uint32(16)), F32)
    hi = lax.bitcast_convert_type(u & jnp.uint32(0xFFFF0000), F32)
    return lo, hi


def _store_packed_rows(ref, x, nrows):
    half = x.shape[1] // 2
    nslab = half // LANES
    for s in range(nslab):
        lo = x[:, s * LANES:(s + 1) * LANES]
        hi = x[:, half + s * LANES:half + (s + 1) * LANES]
        ref[pl.ds(s, nrows, stride=nslab), :] = _pack_bf16_pair(lo, hi)


def _mix_out_kernel(yr_ref, yf_ref, ym_ref, x_ref, eg_ref, eb_ref, woa_ref, wob_ref, woc_ref,
                    g_ref, b_ref, rw_ref, rb_ref, h_ref, hp_ref, ti_ref, tg_ref, *, alpha, n_experts):
    bm, d = x_ref.shape
    h0 = _layer_norm_rows(x_ref[...], eg_ref[...], eb_ref[...], LN_EPS)
    mixed = _dot(yr_ref[...], woa_ref[...]) + _dot(yf_ref[...], wob_ref[...]) + _dot(ym_ref[...], woc_ref[...])
    h1 = _layer_norm_rows(alpha * h0 + mixed, g_ref[...], b_ref[...], LN_EPS)
    h_ref[...] = h1
    _store_packed_rows(hp_ref, h1, bm)

    logits = _dot(h1, rw_ref[...], HIGHEST) + rb_ref[...]
    lane = lax.broadcasted_iota(jnp.int32, logits.shape, 1)
    work = jnp.where(lane < n_experts, logits, -jnp.inf)
    vals, idxs = [], []
    for _ in range(TOP_K):
        m = jnp.max(work, axis=-1, keepdims=True)
        idx = jnp.min(jnp.where(work == m, lane, LANES), axis=-1, keepdims=True)
        work = jnp.where(lane == idx, -jnp.inf, work)
        vals.append(m)
        idxs.append(idx)
    exps = [jnp.exp(v_ - vals[0]) for v_ in vals]
    denom = exps[0] + exps[1] + exps[2] + exps[3]
    ti = jnp.zeros(logits.shape, jnp.int32)
    tg = jnp.zeros(logits.shape, F32)
    for k_ in range(TOP_K):
        ti = jnp.where(lane == k_, idxs[k_], ti)
        tg = jnp.where(lane == k_, exps[k_] / denom, tg)
    ti_ref[...] = ti
    tg_ref[...] = tg


def mix_out(yr, yf, ym, x, emb_g, emb_b, wo, ln_g, ln_b, router_w, router_b, *, alpha, n_experts, bm):
    n, d = x.shape
    wr, wf, wm = yr.shape[1], yf.shape[1], ym.shape[1]
    assert wf == wm and wr % wf == 0 and d % (2 * LANES) == 0
    nslab = d // (2 * LANES)
    row = lambda w_: pl.BlockSpec((bm, w_), lambda i: (i, 0))
    const = lambda r_, w_, ri=0: pl.BlockSpec((r_, w_), lambda i: (ri, 0))
    return pl.pallas_call(
        functools.partial(_mix_out_kernel, alpha=alpha, n_experts=n_experts),
        grid=(n // bm,),
        in_specs=[row(wr), row(wf), row(wm), row(d), const(1, d), const(1, d),
                  const(wr, d), const(wf, d, wr // wf), const(wm, d, wr // wf + 1),
                  const(1, d), const(1, d), const(d, LANES), const(1, LANES)],
        out_specs=[row(d), pl.BlockSpec((bm * nslab, LANES), lambda i: (i, 0)), row(LANES), row(LANES)],
        out_shape=[jax.ShapeDtypeStruct((n, d), F32),
                   jax.ShapeDtypeStruct((n * nslab, LANES), jnp.uint32),
                   jax.ShapeDtypeStruct((n, LANES), jnp.int32),
                   jax.ShapeDtypeStruct((n, LANES), F32)],
        compiler_params=_cparams(("parallel",)),
        name="mix_out",
    )(yr, yf, ym, x, emb_g.reshape(1, d), emb_b.reshape(1, d), wo, wo, wo,
      ln_g.reshape(1, d), ln_b.reshape(1, d), router_w, router_b)


def _moe_kernel(be_ref, nu_ref, tok_ref, h_ref, w1g_ref, w1l_ref, b1g_ref, b1l_ref, w2_ref, b2_ref,
                o_ref, idx_ref, xraw_ref, xb_ref, acc_ref, idx_sem, row_sem, *, nslab):
    bm = xb_ref.shape[0]
    i = pl.program_id(0)
    f = pl.program_id(1)
    nf = pl.num_programs(1)
    n_used = nu_ref[0]

    def fetch_rows(blk):
        cp = pltpu.make_async_copy(tok_ref.at[blk], idx_ref, idx_sem)
        cp.start()
        cp.wait()

        def issue(r, carry):
            src = pl.multiple_of(idx_ref[r] * nslab, nslab)
            dst = pl.multiple_of(r * nslab, nslab)
            pltpu.make_async_copy(h_ref.at[pl.ds(src, nslab), :], xraw_ref.at[pl.ds(dst, nslab), :], row_sem).start()
            return carry

        lax.fori_loop(0, bm, issue, 0, unroll=8)

    @pl.when(i < n_used)
    def _():
        @pl.when(f == 0)
        def _():
            @pl.when(i == 0)
            def _():
                acc_ref[...] = jnp.zeros_like(acc_ref)
                fetch_rows(0)

            pltpu.make_async_copy(h_ref.at[pl.ds(0, bm * nslab), :], xraw_ref, row_sem).wait()
            half = nslab * LANES
            for s in range(nslab):
                lo, hi = _unpack_bf16_pair(xraw_ref[pl.ds(s, bm, stride=nslab), :])
                xb_ref[:, s * LANES:(s + 1) * LANES] = lo.astype(BF16)
                xb_ref[:, half + s * LANES:half + (s + 1) * LANES] = hi.astype(BF16)

            @pl.when(i + 1 < n_used)
            def _():
                fetch_rows(i + 1)

        xb = xb_ref[...]
        hg = _dot(xb, w1g_ref[...].astype(BF16)) + b1g_ref[...]
        hl = _dot(xb, w1l_ref[...].astype(BF16)) + b1l_ref[...]
        glu = jnp.minimum(hg, SWIGLU_LIMIT)
        lin = jnp.clip(hl, -SWIGLU_LIMIT, SWIGLU_LIMIT)
        act = (lin + 1.0) * glu * _sigmoid(SWIGLU_ALPHA * glu)
        acc_ref[...] = jnp.where(f == 0, 0.0, acc_ref[...]) + _dot(act.astype(BF16), w2_ref[...].astype(BF16))

        @pl.when(f == nf - 1)
        def _():
            _store_packed_rows(o_ref, acc_ref[...] + b2_ref[...], bm)

    @pl.when((i >= n_used) & (f == nf - 1))
    def _():
        o_ref[...] = jnp.zeros_like(o_ref)


def moe_experts(block_e, n_used, tok_blocks, h_rows, w1, b1, w2, b2, *, bm, tf):
    nblk = tok_blocks.shape[0]
    n_exp, d, de2 = w1.shape
    de = de2 // 2
    nslab = d // (2 * LANES)
    nf = de // tf

    def blk(i, nu):
        return jnp.minimum(i, nu[0] - 1)

    def fi(i, f, nu):
        return jnp.where(i < nu[0], f, nf - 1)

    grid_spec = pltpu.PrefetchScalarGridSpec(
        num_scalar_prefetch=2,
        grid=(nblk, nf),
        in_specs=[
            pl.BlockSpec(memory_space=pl.ANY),
            pl.BlockSpec(memory_space=pl.ANY),
            pl.BlockSpec((None, d, tf), lambda i, f, be, nu: (be[blk(i, nu)], 0, fi(i, f, nu))),
            pl.BlockSpec((None, d, tf), lambda i, f, be, nu: (be[blk(i, nu)], 0, nf + fi(i, f, nu))),
            pl.BlockSpec((None, 1, tf), lambda i, f, be, nu: (be[blk(i, nu)], 0, fi(i, f, nu))),
            pl.BlockSpec((None, 1, tf), lambda i, f, be, nu: (be[blk(i, nu)], 0, nf + fi(i, f, nu))),
            pl.BlockSpec((None, tf, d), lambda i, f, be, nu: (be[blk(i, nu)], fi(i, f, nu), 0)),
            pl.BlockSpec((None, 1, d), lambda i, f, be, nu: (be[blk(i, nu)], 0, 0)),
        ],
        out_specs=pl.BlockSpec((bm * nslab, LANES), lambda i, f, be, nu: (i, 0)),
        scratch_shapes=[
            pltpu.SMEM((bm,), jnp.int32),
            pltpu.VMEM((bm * nslab, LANES), jnp.uint32),
            pltpu.VMEM((bm, d), BF16),
            pltpu.VMEM((bm, d), F32),
            pltpu.SemaphoreType.DMA,
            pltpu.SemaphoreType.DMA,
        ],
    )
    return pl.pallas_call(
        functools.partial(_moe_kernel, nslab=nslab),
        grid_spec=grid_spec,
        out_shape=jax.ShapeDtypeStruct((nblk * bm * nslab, LANES), jnp.uint32),
        compiler_params=_cparams(("arbitrary", "arbitrary")),
        name="moe_experts",
    )(block_e, n_used, tok_blocks, h_rows, w1, w1, b1, b1, w2, b2)


def _combine_kernel(pos_ref, y_ref, h_ref, tg_ref, g_ref, b_ref, o_ref, idx_ref, ybuf_ref, idx_sem, row_sem,
                    *, alpha, nslab):
    tm = o_ref.shape[0]
    nrow = TOP_K * tm
    i = pl.program_id(0)
    n_steps = pl.num_programs(0)

    def fetch_rows(step, slot):
        cp = pltpu.make_async_copy(pos_ref.at[step], idx_ref.at[slot], idx_sem)
        cp.start()
        cp.wait()

        def issue(r, carry):
            src = pl.multiple_of(idx_ref[slot, r] * nslab, nslab)
            dst = pl.multiple_of(r * nslab, nslab)
            pltpu.make_async_copy(y_ref.at[pl.ds(src, nslab), :], ybuf_ref.at[slot, pl.ds(dst, nslab), :],
                                  row_sem.at[slot]).start()
            return carry

        lax.fori_loop(0, nrow, issue, 0)

    slot = i % 2

    @pl.when(i == 0)
    def _():
        fetch_rows(0, 0)

    @pl.when(i + 1 < n_steps)
    def _():
        fetch_rows(i + 1, 1 - slot)

    pltpu.make_async_copy(y_ref.at[pl.ds(0, nrow * nslab), :], ybuf_ref.at[slot], row_sem.at[slot]).wait()

    gates = [tg_ref[:, k_:k_ + 1] for k_ in range(TOP_K)]
    lo_slabs, hi_slabs = [], []
    for s in range(nslab):
        acc_lo = acc_hi = None
        for k_ in range(TOP_K):
            lo, hi = _unpack_bf16_pair(ybuf_ref[slot, pl.ds(k_ * tm * nslab + s, tm, stride=nslab), :])
            acc_lo = gates[k_] * lo if acc_lo is None else acc_lo + gates[k_] * lo
            acc_hi = gates[k_] * hi if acc_hi is None else acc_hi + gates[k_] * hi
        lo_slabs.append(acc_lo)
        hi_slabs.append(acc_hi)
    pre = alpha * h_ref[...] + jnp.concatenate(lo_slabs + hi_slabs, axis=-1)
    o_ref[...] = _layer_norm_rows(pre, g_ref[...], b_ref[...], LN_EPS).astype(o_ref.dtype)


def moe_combine(pos_blocks, y_rows, h1, gates, ln_g, ln_b, *, alpha, tm, out_dtype):
    n = gates.shape[0]
    d = ln_g.shape[-1]
    nslab = d // (2 * LANES)
    nrow = TOP_K * tm
    return pl.pallas_call(
        functools.partial(_combine_kernel, alpha=alpha, nslab=nslab),
        grid=(n // tm,),
        in_specs=[pl.BlockSpec(memory_space=pl.ANY),
                  pl.BlockSpec(memory_space=pl.ANY),
                  pl.BlockSpec((tm, d), lambda i: (i, 0)),
                  pl.BlockSpec((tm, LANES), lambda i: (i, 0)),
                  pl.BlockSpec((1, d), lambda i: (0, 0)),
                  pl.BlockSpec((1, d), lambda i: (0, 0))],
        out_specs=pl.BlockSpec((tm, d), lambda i: (i, 0)),
        out_shape=jax.ShapeDtypeStruct((n, d), out_dtype),
        scratch_shapes=[pltpu.SMEM((2, nrow), jnp.int32),
                        pltpu.VMEM((2, nrow * nslab, LANES), jnp.uint32),
                        pltpu.SemaphoreType.DMA,
                        pltpu.SemaphoreType.DMA((2,))],
        compiler_params=_cparams(("arbitrary",)),
        name="moe_combine",
    )(pos_blocks, y_rows, h1, gates, ln_g.reshape(1, d), ln_b.reshape(1, d))


def _route(top_idx, n_experts, bm, tm):
    n = top_idx.shape[0]
    n_assign = n * TOP_K
    e = top_idx.reshape(-1)
    onehot = (e[:, None] == jnp.arange(n_experts, dtype=jnp.int32)[None, :]).astype(jnp.int32)
    csum = jnp.cumsum(onehot, axis=0)
    rank = jnp.sum((csum - onehot) * onehot, axis=-1)
    counts = csum[-1]
    padded = ((counts + bm - 1) // bm) * bm
    pend = jnp.cumsum(padded)
    pstart = pend - padded
    pos = jnp.sum(onehot * pstart[None, :], axis=-1) + rank
    nblk = n_assign // bm + n_experts
    tok = jnp.arange(n_assign, dtype=jnp.int32) // TOP_K
    buf_tok = jnp.zeros((nblk * bm,), jnp.int32).at[pos].set(tok)
    block_e = jnp.clip(jnp.searchsorted(pend, jnp.arange(nblk, dtype=jnp.int32) * bm, side="right"),
                       0, n_experts - 1).astype(jnp.int32)
    n_used = (pend[-1] // bm).astype(jnp.int32).reshape(1)
    pos_blocks = pos.reshape(n // tm, tm, TOP_K).transpose(0, 2, 1).reshape(n // tm, TOP_K * tm)
    return block_e, n_used, buf_tok.reshape(nblk, bm), pos_blocks.astype(jnp.int32)


RWKV_HEADS = 16
DECAY_LORA = 96
ICLR_LORA = 96
GATE_LORA = 256
FOX_HEADS = 8
MEM_HEADS = 4
MEM_HEAD_DIM = 128
N_EXPERTS = 32
LORA_PAD = 128

TILES = dict(proj_bm=1024, proj_bn=896, qkvm_bn=1024, rwkv_tt=1024, rwkv_gw=256, fox_tt=512, fox_bq=1024, fox_sub=256, mem_tq=1024,
             mix_bm=512, moe_bm=1024, moe_tf=256, comb_tm=256)


def _bf16_floor(a):
    bits = lax.bitcast_convert_type(a, jnp.uint32) & jnp.uint32(0xFFFF0000)
    return lax.bitcast_convert_type(bits, F32)


def _pad_cols(a, width):
    return jnp.pad(a, [(0, 0)] * (a.ndim - 1) + [(0, width - a.shape[-1])])


def _layer(x, mem_n_kv, emb_ln_g, emb_ln_b, w_in, mu, w0, wd, a0, wa, wg, k_k, k_a, r_k, gn_g, gn_b, b_f,
           w_o, ln1_g, ln1_b, router_w, router_b, w1, b1, w2, b2, ln2_g, ln2_b, alpha, tiles):
    bsz, t_len, d = x.shape
    n = bsz * t_len
    wr = RWKV_HEADS * HEAD64
    wf = FOX_HEADS * HEAD64
    wm = MEM_HEADS * MEM_HEAD_DIM
    c0 = 3 * wr
    c1 = c0 + DECAY_LORA
    c2 = c1 + ICLR_LORA
    c3 = c2 + GATE_LORA
    f0 = c3 + 3 * wf
    f1 = f0 + FOX_HEADS
    tl = lambda name, dim: min(tiles[name], dim)

    def rwkv_cols(a):
        return jnp.concatenate([a[..., :c0], _pad_cols(a[..., c0:c1], LORA_PAD),
                                _pad_cols(a[..., c1:c2], LORA_PAD), a[..., c2:c3]], axis=-1)

    w_rwkv = rwkv_cols(w_in).astype(BF16)
    w_qkvm = jnp.concatenate([w_in[:, c3:f0], w_in[:, f1:]], axis=-1).astype(BF16)
    w_f = _pad_cols(w_in[:, f0:f1], LANES).astype(BF16)
    x2 = x.reshape(n, d)
    bm = tl("proj_bm", n)
    p_rwkv = ln_matmul(x2, emb_ln_g, emb_ln_b, w_rwkv, F32, bm, tl("proj_bn", w_rwkv.shape[1]))
    p_qkvm = ln_matmul(x2, emb_ln_g, emb_ln_b, w_qkvm, BF16, bm, tl("qkvm_bn", w_qkvm.shape[1]))
    f_log = ln_matmul(x2, emb_ln_g, emb_ln_b, w_f, F32, bm, LANES)

    zrows = lambda a, rows: jnp.pad(a, ((0, rows - a.shape[0]), (0, 0)))
    y_rwkv = rwkv_mix(p_rwkv.reshape(bsz, t_len, -1), rwkv_cols(mu), w0, a0, k_k, k_a, r_k.reshape(-1), gn_g, gn_b,
                      zrows(wd, LORA_PAD), zrows(wa, LORA_PAD), wg,
                      width=wr, tt=tl("rwkv_tt", t_len), gw=tl("rwkv_gw", wr), out_dtype=BF16)

    c = fox_cumsum(f_log.reshape(bsz, t_len, LANES), _pad_cols(b_f.reshape(1, -1), LANES), tl("fox_tt", t_len))
    bh = bsz * FOX_HEADS
    bq = tl("fox_bq", t_len)
    c_rows = c[..., :FOX_HEADS].transpose(0, 2, 1).reshape(bh, 1, t_len)
    bias = (c_rows.reshape(bh, 1, t_len // bq, bq)[..., :1] - c_rows.reshape(bh, 1, t_len // bq, bq)).reshape(bh, 1, t_len)
    b_hi = _bf16_floor(bias)
    b_mid = _bf16_floor(bias - b_hi)
    b_lo = bias - b_hi - b_mid
    b_hi, b_mid, b_lo = b_hi.astype(BF16), b_mid.astype(BF16), b_lo.astype(BF16)
    qkv = p_qkvm[:, :3 * wf].reshape(bsz, t_len, 3, FOX_HEADS, HEAD64)
    qkv = qkv.transpose(2, 0, 3, 1, 4).reshape(3, bh, t_len, HEAD64)
    q_scaled = (qkv[0].astype(F32) * (HEAD64 ** -0.5 * LOG2E)).astype(BF16)
    q_aug = jnp.concatenate([q_scaled, jnp.ones((bh, t_len, 3), BF16),
                             jnp.zeros((bh, t_len, LANES - HEAD64 - 3), BF16)], axis=-1)
    kt_aug = jnp.concatenate([qkv[1].transpose(0, 2, 1), b_hi, b_mid, b_lo,
                              jnp.zeros((bh, LANES - HEAD64 - 3, t_len), BF16)], axis=1)
    v_aug = jnp.concatenate([qkv[2], jnp.ones((bh, t_len, 1), BF16),
                             jnp.zeros((bh, t_len, LANES - HEAD64 - 1), BF16)], axis=-1)
    o_fox = fox_attention(q_aug, kt_aug, v_aug, c_rows, bq=bq, sub=tl("fox_sub", bq), dh=HEAD64, out_dtype=BF16)
    y_fox = o_fox.reshape(bsz, FOX_HEADS, t_len, HEAD64).transpose(0, 2, 1, 3).reshape(n, wf)

    m_len = mem_n_kv.shape[0] // bsz
    km = mem_n_kv[:, :wm].reshape(bsz, m_len, MEM_HEADS, MEM_HEAD_DIM).transpose(0, 2, 3, 1).astype(BF16)
    vm = mem_n_kv[:, wm:].reshape(bsz, m_len, MEM_HEADS, MEM_HEAD_DIM).transpose(0, 2, 1, 3).astype(BF16)
    y_mem = mem_attention(p_qkvm.reshape(bsz, t_len, -1), (3 * wf) // wm, km, vm, heads=MEM_HEADS, width=wm,
                          tq=tl("mem_tq", t_len), out_dtype=BF16)

    rw = _pad_cols(router_w, LANES)
    rb = _pad_cols(router_b.reshape(1, -1), LANES)
    h1, h_rows, top_i, top_g = mix_out(y_rwkv.reshape(n, wr), y_fox, y_mem.reshape(n, wm), x2, emb_ln_g, emb_ln_b,
                                   w_o.astype(BF16), ln1_g, ln1_b, rw, rb, alpha=alpha, n_experts=N_EXPERTS,
                                   bm=tl("mix_bm", n))

    moe_bm = tl("moe_bm", n)
    tm = tl("comb_tm", n)
    block_e, n_used, tok_blocks, pos_blocks = _route(top_i[:, :TOP_K], N_EXPERTS, moe_bm, tm)
    y_rows = moe_experts(block_e, n_used, tok_blocks, h_rows, w1, b1[:, None, :], w2, b2[:, None, :],
                         bm=moe_bm, tf=tl("moe_tf", w2.shape[1]))
    out = moe_combine(pos_blocks, y_rows, h1, top_g, ln2_g, ln2_b, alpha=alpha, tm=tm, out_dtype=x.dtype)
    return out.reshape(bsz, t_len, d)


def kernel(x, mem, emb_ln_g, emb_ln_b, w_in, rwkv_mu, rwkv_w0, rwkv_w_decay_up, rwkv_a0, rwkv_w_iclr_up,
           rwkv_w_gate_up, rwkv_k_k, rwkv_k_a, rwkv_r_k, rwkv_gn_g, rwkv_gn_b, fox_b_f, mem_ln_g, mem_ln_b,
           w_mem_kv, w_o, ln1_g, ln1_b, router_w, router_b, expert_w1, expert_b1, expert_w2, expert_b2,
           ln2_g, ln2_b):
    depth = w_in.shape[0]
    assert depth == 1
    alpha = (2 * depth) ** 0.25
    bsz, m_len, d = mem.shape
    kv = ln_matmul(mem.reshape(bsz * m_len, d), mem_ln_g, mem_ln_b, w_mem_kv[0].astype(BF16), F32,
                   min(512, bsz * m_len), w_mem_kv.shape[-1])
    return _layer(x, kv, emb_ln_g, emb_ln_b, w_in[0], rwkv_mu[0], rwkv_w0[0], rwkv_w_decay_up[0], rwkv_a0[0],
                  rwkv_w_iclr_up[0], rwkv_w_gate_up[0], rwkv_k_k[0], rwkv_k_a[0], rwkv_r_k[0], rwkv_gn_g[0],
                  rwkv_gn_b[0], fox_b_f[0], w_o[0], ln1_g[0], ln1_b[0], router_w[0], router_b[0],
                  expert_w1[0], expert_b1[0], expert_w2[0], expert_b2[0], ln2_g[0], ln2_b[0], alpha, TILES)
```

```python
import functools
import math

import jax
import jax.numpy as jnp
from jax import lax
from jax.experimental import pallas as pl
from jax.experimental.pallas import tpu as pltpu

F32 = jnp.float32
BF16 = jnp.bfloat16
HIGHEST = lax.Precision.HIGHEST

LANES = 128
HEAD64 = 64
VMEM_LIMIT = 56 * 1024 * 1024

LN_EPS = 1e-5
RWKV_GN_EPS = 64e-5
SWIGLU_LIMIT = 7.0
SWIGLU_ALPHA = 1.702
TOP_K = 4
LOG2E = 1.4426950408889634


def _cparams(sem):
    return pltpu.CompilerParams(dimension_semantics=sem, vmem_limit_bytes=VMEM_LIMIT)


def _dot(a, b, precision=None):
    return jnp.dot(a, b, preferred_element_type=F32, precision=precision)


def _dot_nt(a, b, precision=None):
    return lax.dot_general(a, b, (((1,), (1,)), ((), ())), preferred_element_type=F32, precision=precision)


def _dot_tn(a, b, precision=None):
    return lax.dot_general(a, b, (((0,), (0,)), ((), ())), preferred_element_type=F32, precision=precision)


def _dot_split(a, b, split_rhs=False):
    x = b if split_rhs else a
    hi = x.astype(BF16)
    lo = (x - hi.astype(F32)).astype(BF16)
    if split_rhs:
        return _dot(a, hi) + _dot(a, lo)
    return _dot(hi, b) + _dot(lo, b)


def _dot_bf16x3(a, b):
    a_hi = a.astype(BF16)
    a_lo = (a - a_hi.astype(F32)).astype(BF16)
    b_hi = b.astype(BF16)
    b_lo = (b - b_hi.astype(F32)).astype(BF16)
    return _dot(a_hi, b_hi) + _dot(a_lo, b_hi) + _dot(a_hi, b_lo)


def _layer_norm_rows(x, g, b, eps):
    mean = jnp.mean(x, axis=-1, keepdims=True)
    xc = x - mean
    var = jnp.mean(xc * xc, axis=-1, keepdims=True)
    return xc * lax.rsqrt(var + eps) * g + b


def _sigmoid(x):
    return 1.0 / (1.0 + jnp.exp(-x))


def _ln_matmul_kernel(x_ref, g_ref, b_ref, w_ref, *refs, starts):
    o_refs, xn_ref = refs[:-1], refs[-1]
    j = pl.program_id(1)

    @pl.when(j == 0)
    def _():
        xn_ref[...] = _layer_norm_rows(x_ref[...], g_ref[...], b_ref[...], LN_EPS).astype(BF16)

    res = _dot(xn_ref[...], w_ref[...])
    for k, o_ref in enumerate(o_refs):
        @pl.when((j >= starts[k]) & (j < starts[k + 1]))
        def _(o_ref=o_ref):
            o_ref[...] = res.astype(o_ref.dtype)


def ln_matmul(x, g, b, ws, out_dtypes, bm, bn):
    n, d = x.shape
    assert n % bm == 0 and all(w.shape[1] % bn == 0 for w in ws)
    tiles = [w.shape[1] // bn for w in ws]
    starts = [sum(tiles[:k]) for k in range(len(ws) + 1)]

    def out_spec(k):
        return pl.BlockSpec((bm, bn), lambda i, j: (i, jnp.clip(j - starts[k], 0, tiles[k] - 1)))

    return pl.pallas_call(
        functools.partial(_ln_matmul_kernel, starts=tuple(starts)),
        grid=(n // bm, starts[-1]),
        in_specs=[
            pl.BlockSpec((bm, d), lambda i, j: (i, 0)),
            pl.BlockSpec((1, d), lambda i, j: (0, 0)),
            pl.BlockSpec((1, d), lambda i, j: (0, 0)),
            pl.BlockSpec((d, bn), lambda i, j: (0, j)),
        ],
        out_specs=[out_spec(k) for k in range(len(ws))],
        out_shape=[jax.ShapeDtypeStruct((n, w.shape[1]), dt) for w, dt in zip(ws, out_dtypes)],
        scratch_shapes=[pltpu.VMEM((bm, d), BF16)],
        compiler_params=_cparams(("parallel", "arbitrary")),
        name="ln_matmul",
    )(x, g.reshape(1, d), b.reshape(1, d), jnp.concatenate(ws, axis=1) if len(ws) > 1 else ws[0])


RWKV_CHUNK = 64


def _rwkv_kernel(r_ref, k_ref, v_ref, dw_ref, da_ref, dg_ref,
                 mur_ref, muk_ref, muv_ref, mudw_ref, muda_ref, mudg_ref,
                 w0_ref, a0_ref, kk_ref, ka_ref, rk_ref, gng_ref, gnb_ref,
                 wd_ref, wa_ref, wg_ref,
                 o_ref,
                 cr_ref, ck_ref, cv_ref, cdw_ref, cda_ref, cdg_ref, h_ref,
                 sr_ref, slw_ref, sk_ref, sv_ref, sa_ref, sb_ref, sy_ref,
                 rp_ref, y0_ref, m_ref, ha_ref):
    tt, gw = r_ref.shape
    nh = gw // HEAD64
    C = RWKV_CHUNK
    S = nh * C

    @pl.when(pl.program_id(2) == 0)
    def _():
        for c in (cr_ref, ck_ref, cv_ref, cdw_ref, cda_ref, cdg_ref, h_ref):
            c[...] = jnp.zeros_like(c)

    def shift(x_ref, mu_ref, carry_ref):
        x = x_ref[...]
        rows = lax.broadcasted_iota(jnp.int32, x.shape, 0)
        prev = jnp.where(rows == 0, carry_ref[0:1, :], pltpu.roll(x, 1, axis=0))
        carry_ref[0:1, :] = x[tt - 1:tt, :]
        return x + mu_ref[...] * (prev - x)

    r = shift(r_ref, mur_ref, cr_ref)
    k = shift(k_ref, muk_ref, ck_ref)
    v = shift(v_ref, muv_ref, cv_ref)
    dw = shift(dw_ref, mudw_ref, cdw_ref)
    da = shift(da_ref, muda_ref, cda_ref)
    dg = shift(dg_ref, mudg_ref, cdg_ref)

    lane = lax.broadcasted_iota(jnp.int32, (gw, gw), 1)
    row = lax.broadcasted_iota(jnp.int32, (gw, gw), 0)
    seg_ones = jnp.where((lane // HEAD64) == (row // HEAD64), 1.0, 0.0).astype(BF16)

    def segsum(x):
        return _dot_split(x, seg_ones)

    bdot = lambda a_, b_: _dot(a_.astype(BF16), b_.astype(BF16))
    z = w0_ref[...] + bdot(jnp.tanh(dw), wd_ref[...])
    softplus_neg = jnp.maximum(-z, 0.0) + jnp.log(1.0 + jnp.exp(-jnp.abs(z)))
    lw = -jnp.exp(-softplus_neg - 0.5)
    a = _sigmoid(a0_ref[...] + bdot(da, wa_ref[...]))
    gate = bdot(_sigmoid(dg), wg_ref[...])
    kk = k * kk_ref[...]
    kk = kk / jnp.maximum(jnp.sqrt(segsum(kk * kk)), 1e-12)
    k2 = k * (1.0 + (a - 1.0) * ka_ref[...])
    bonus = segsum(r * k2 * rk_ref[...]) * v

    sr_ref[...] = r
    slw_ref[...] = lw
    sk_ref[...] = k2
    sv_ref[...] = v
    sa_ref[...] = -kk
    sb_ref[...] = kk * a

    head_of_lane = lax.broadcasted_iota(jnp.int32, (C, gw), 1) // HEAD64
    rr = lax.broadcasted_iota(jnp.int32, (S, S), 0)
    cc = lax.broadcasted_iota(jnp.int32, (S, S), 1)
    same_head = (rr // C) == (cc // C)
    strict = same_head & ((rr % C) > (cc % C))
    incl = same_head & ((rr % C) >= (cc % C))
    eye_s = jnp.where(rr == cc, 1.0, 0.0).astype(F32)
    rc = lax.broadcasted_iota(jnp.int32, (C, C), 0)
    cc1 = lax.broadcasted_iota(jnp.int32, (C, C), 1)
    tri_incl = jnp.where(rc >= cc1, 1.0, 0.0).astype(BF16)
    diag_mask = row == lane

    def stack(x):
        return jnp.concatenate([jnp.where(head_of_lane == h_, x, 0.0) for h_ in range(nh)], axis=0)

    def chunk_terms(g, carry):
        cs = [g * group + i for i in range(group)]
        each = lambda fn, *lists: [fn(*args) for args in zip(*lists)]
        sls = [pl.ds(pl.multiple_of(c * C, C), C) for c in cs]
        sqs = [pl.ds(pl.multiple_of(c * S, S), S) for c in cs]
        sgs = [pl.ds(pl.multiple_of(c * gw, gw), gw) for c in cs]
        lws = [slw_ref[sl, :] for sl in sls]
        lgs = each(lambda lw_: _dot_split(tri_incl, lw_, split_rhs=True), lws)
        lasts = [lg[C - 1:C, :] for lg in lgs]
        e_negs = [jnp.exp(-lg) for lg in lgs]
        e_ends = each(lambda lg, last: jnp.exp(last - lg), lgs, lasts)
        rts = each(lambda sl, lg: stack(sr_ref[sl, :] * jnp.exp(lg)), sls, lgs)
        ats = each(lambda sl, lg, lw_: stack(sa_ref[sl, :] * jnp.exp(lg - lw_)).astype(BF16), sls, lgs, lws)
        bts = each(lambda sl, e: stack(sb_ref[sl, :] * e).astype(BF16), sls, e_negs)
        kts = each(lambda sl, e: stack(sk_ref[sl, :] * e).astype(BF16), sls, e_negs)
        bhs = each(lambda sl, e: stack(sb_ref[sl, :] * e).astype(BF16), sls, e_ends)
        khs = each(lambda sl, e: stack(sk_ref[sl, :] * e).astype(BF16), sls, e_ends)
        vss = [stack(sv_ref[sl, :]).astype(BF16) for sl in sls]

        quads = each(lambda at, rt, bt, kt: _dot_nt(jnp.concatenate([at, rt.astype(BF16)], axis=0),
                                                    jnp.concatenate([bt, kt], axis=0)), ats, rts, bts, kts)
        a_abs = [jnp.where(strict, q_[:S, :S], 0.0) for q_ in quads]
        a_aks = [jnp.where(strict, q_[:S, S:], 0.0).astype(BF16) for q_ in quads]
        a_rbs = [jnp.where(incl, q_[S:, :S], 0.0).astype(BF16) for q_ in quads]
        a_rks = [jnp.where(incl, q_[S:, S:], 0.0).astype(BF16) for q_ in quads]

        pws = [_dot(n_.astype(BF16), n_.astype(BF16)) for n_ in a_abs]
        invs = [eye_s + n_ for n_ in a_abs]
        n_steps = int(math.log2(C)) - 1
        for step in range(n_steps):
            pwbs = [pw.astype(BF16) for pw in pws]
            if step + 1 < n_steps:
                boths = each(lambda pwb, inv: _dot(jnp.concatenate([pwb, inv.astype(BF16)], axis=0), pwb), pwbs, invs)
                pws = [b_[:S, :] for b_ in boths]
                invs = each(lambda inv, b_: inv + b_[S:, :], invs, boths)
            else:
                invs = each(lambda inv, pwb: inv + _dot(inv.astype(BF16), pwb), invs, pwbs)
        invbs = [inv.astype(BF16) for inv in invs]

        akvs = each(lambda a_ak, vs: _dot(a_ak, vs).astype(BF16), a_aks, vss)
        trs = each(lambda invb, at, akv: _dot(invb, jnp.concatenate([at, akv], axis=1)), invbs, ats, akvs)
        atps = [tr[:, :gw].astype(BF16) for tr in trs]
        uvs = each(lambda tr, vs: jnp.concatenate([tr[:, gw:].astype(BF16), vs], axis=0), trs, vss)
        for i in range(group):
            rp_ref[sqs[i], :] = (rts[i] + _dot(a_rbs[i], atps[i])).astype(BF16)
            y0_ref[sqs[i], :] = _dot(jnp.concatenate([a_rbs[i], a_rks[i]], axis=1), uvs[i])
            m_ref[sgs[i], :] = (jnp.where(diag_mask, jnp.exp(lasts[i]), 0.0) + _dot_tn(bhs[i], atps[i])).astype(BF16)
            ha_ref[sgs[i], :] = _dot_tn(jnp.concatenate([bhs[i], khs[i]], axis=0), uvs[i])
        return carry

    group = math.gcd(tt // C, 4)
    lax.fori_loop(0, tt // (C * group), chunk_terms, 0)

    def chunk_state(c, carry):
        sl = pl.ds(pl.multiple_of(c * C, C), C)
        sq = pl.ds(pl.multiple_of(c * S, S), S)
        sg = pl.ds(pl.multiple_of(c * gw, gw), gw)
        h0 = h_ref[...]
        y = y0_ref[sq, :] + _dot_split(rp_ref[sq, :], h0, split_rhs=True)
        y_sum = y[0:C, :]
        for h_ in range(1, nh):
            y_sum = y_sum + y[h_ * C:(h_ + 1) * C, :]
        sy_ref[sl, :] = y_sum
        h_ref[...] = ha_ref[sg, :] + _dot_split(m_ref[sg, :], h0, split_rhs=True)
        return carry

    lax.fori_loop(0, tt // C, chunk_state, 0)

    y = sy_ref[...]
    mean = segsum(y) * (1.0 / HEAD64)
    yc = y - mean
    var = segsum(yc * yc) * (1.0 / HEAD64)
    yn = yc * lax.rsqrt(var + RWKV_GN_EPS) * gng_ref[...] + gnb_ref[...]
    o_ref[...] = ((yn + bonus) * gate).astype(o_ref.dtype)


def rwkv_mix(p, mu, w0, a0, k_k, k_a, r_k, gn_g, gn_b, wd, wa, wg, *, width, tt, gw, out_dtype):
    bsz, t_len, _ = p.shape
    assert gw % LANES == 0 and width % gw == 0 and (3 * width) % 256 == 0
    assert t_len % tt == 0 and tt % RWKV_CHUNK == 0
    ngrp = width // gw
    col_dw = 3 * width // LANES
    col_da = col_dw + 1
    col_dg = (3 * width + 256) // 256
    nchunk = tt // RWKV_CHUNK
    stacked_rows = (gw // HEAD64) * RWKV_CHUNK

    def tok(width_, col):
        return pl.BlockSpec((None, tt, width_), lambda b, h, t: (b, t, col(h)))

    def par(rows, width_, col):
        return pl.BlockSpec((rows, width_), lambda b, h, t: (0, col(h)))

    vec = lambda a_: a_.reshape(1, -1)
    cols = [lambda h: h, lambda h: ngrp + h, lambda h: 2 * ngrp + h,
            lambda h: col_dw, lambda h: col_da, lambda h: col_dg]
    widths = [gw, gw, gw, LANES, LANES, 256]
    in_specs = [tok(w_, c_) for w_, c_ in zip(widths, cols)]
    in_specs += [par(1, w_, c_) for w_, c_ in zip(widths, cols)]
    in_specs += [par(1, gw, lambda h: h)] * 7
    in_specs += [par(LANES, gw, lambda h: h), par(LANES, gw, lambda h: h), par(256, gw, lambda h: h)]
    mu2 = vec(mu)
    args = [p] * 6 + [mu2] * 6 + [vec(w0), vec(a0), vec(k_k), vec(k_a), vec(r_k), vec(gn_g), vec(gn_b), wd, wa, wg]
    carry = lambda w_: pltpu.VMEM((8, w_), F32)
    tile = pltpu.VMEM((tt, gw), F32)
    return pl.pallas_call(
        _rwkv_kernel,
        grid=(bsz, ngrp, t_len // tt),
        in_specs=in_specs,
        out_specs=pl.BlockSpec((None, tt, gw), lambda b, h, t: (b, t, h)),
        out_shape=jax.ShapeDtypeStruct((bsz, t_len, width), out_dtype),
        scratch_shapes=([carry(gw)] * 3 + [carry(LANES)] * 2 + [carry(256), pltpu.VMEM((gw, gw), F32)]
                        + [tile] * 7
                        + [pltpu.VMEM((nchunk * stacked_rows, gw), BF16), pltpu.VMEM((nchunk * stacked_rows, gw), F32)]
                        + [pltpu.VMEM((nchunk * gw, gw), BF16), pltpu.VMEM((nchunk * gw, gw), F32)]),
        compiler_params=_cparams(("parallel", "parallel", "arbitrary")),
        name="rwkv_mix",
    )(*args)


def _fox_cumsum_kernel(f_ref, bf_ref, o_ref, carry_ref):
    @pl.when(pl.program_id(1) == 0)
    def _():
        carry_ref[...] = jnp.zeros_like(carry_ref)

    tt = f_ref.shape[0]
    z = f_ref[...] + bf_ref[...]
    log_f = -LOG2E * (jnp.maximum(-z, 0.0) + jnp.log(1.0 + jnp.exp(-jnp.abs(z))))
    rr = lax.broadcasted_iota(jnp.int32, (tt, tt), 0)
    cc = lax.broadcasted_iota(jnp.int32, (tt, tt), 1)
    tri = jnp.where(rr >= cc, 1.0, 0.0).astype(F32)
    c = _dot(tri, log_f, HIGHEST) + carry_ref[0:1, :]
    o_ref[...] = c
    carry_ref[0:1, :] = c[tt - 1:tt, :]


def fox_cumsum(f_logits, b_f, tt):
    bsz, t_len, _ = f_logits.shape
    return pl.pallas_call(
        _fox_cumsum_kernel,
        grid=(bsz, t_len // tt),
        in_specs=[pl.BlockSpec((None, tt, LANES), lambda b, t: (b, t, 0)),
                  pl.BlockSpec((1, LANES), lambda b, t: (0, 0))],
        out_specs=pl.BlockSpec((None, tt, LANES), lambda b, t: (b, t, 0)),
        out_shape=jax.ShapeDtypeStruct((bsz, t_len, LANES), F32),
        scratch_shapes=[pltpu.VMEM((8, LANES), F32)],
        compiler_params=_cparams(("parallel", "arbitrary")),
        name="fox_cumsum",
    )(f_logits, b_f)


def _fox_attn_kernel(q_ref, kt_ref, v_ref, c_ref, o_ref, m_ref, acc_ref, *, sub, dh):
    bq = q_ref.shape[0]
    nsub = bq // sub
    qi = pl.program_id(1)
    q_start = pl.multiple_of(qi * bq, bq)
    c_q0 = c_ref[:, pl.ds(q_start, LANES)][:, 0:1]
    m_ref[...] = jnp.full_like(m_ref, -jnp.inf)
    acc_ref[...] = jnp.zeros_like(acc_ref)

    def block(k_start, widths, shift, masked):
        scores = [_dot(q_ref[r * sub:(r + 1) * sub, :], kt_ref[:, pl.ds(k_start, widths[r])]) for r in range(nsub)]
        probs, corrs = [], []
        for r in range(nsub):
            rows = slice(r * sub, (r + 1) * sub)
            s = scores[r]
            if masked:
                rr = lax.broadcasted_iota(jnp.int32, s.shape, 0) + r * sub
                cc = lax.broadcasted_iota(jnp.int32, s.shape, 1)
                s = jnp.where(cc <= rr, s, -jnp.inf)
            m_old = m_ref[rows, :]
            m_new = jnp.maximum(m_old, jnp.max(s, axis=-1, keepdims=True) + shift)
            probs.append(jnp.exp2(s - (m_new - shift)).astype(BF16))
            corrs.append(jnp.exp2(m_old - m_new))
            m_ref[rows, :] = m_new
        for r in range(nsub):
            rows = slice(r * sub, (r + 1) * sub)
            acc_ref[rows, :] = acc_ref[rows, :] * corrs[r] + _dot(probs[r], v_ref[pl.ds(k_start, widths[r]), :])

    def body(j, carry):
        k_start = pl.multiple_of(j * bq, bq)
        shift = c_q0 - c_ref[:, pl.ds(k_start, LANES)][:, 0:1]
        block(k_start, [bq] * nsub, shift, False)
        return carry

    lax.fori_loop(0, qi, body, 0)
    block(q_start, [(r + 1) * sub for r in range(nsub)], 0.0, True)
    acc = acc_ref[...]
    o_ref[...] = (acc[:, :dh] / acc[:, dh:dh + 1]).astype(o_ref.dtype)


def fox_attention(q_aug, kt_aug, v_aug, c2, *, bq, sub, dh, out_dtype):
    bh, t_len, _ = q_aug.shape
    return pl.pallas_call(
        functools.partial(_fox_attn_kernel, sub=sub, dh=dh),
        grid=(bh, t_len // bq),
        in_specs=[pl.BlockSpec((None, bq, LANES), lambda b, i: (b, i, 0)),
                  pl.BlockSpec((None, LANES, t_len), lambda b, i: (b, 0, 0)),
                  pl.BlockSpec((None, t_len, LANES), lambda b, i: (b, 0, 0)),
                  pl.BlockSpec((None, 1, t_len), lambda b, i: (b, 0, 0))],
        out_specs=pl.BlockSpec((None, bq, dh), lambda b, i: (b, i, 0)),
        out_shape=jax.ShapeDtypeStruct((bh, t_len, dh), out_dtype),
        scratch_shapes=[pltpu.VMEM((bq, 1), F32), pltpu.VMEM((bq, LANES), F32)],
        compiler_params=_cparams(("parallel", "arbitrary")),
        name="fox_attention",
    )(q_aug, kt_aug, v_aug, c2)


def _mem_attn_kernel(q_ref, kt_ref, v_ref, o_ref, *, heads, scale):
    dh = q_ref.shape[1] // heads
    outs = []
    for h in range(heads):
        q = q_ref[:, h * dh:(h + 1) * dh]
        s = _dot(q, kt_ref[h]) * scale
        m = jnp.max(s, axis=-1, keepdims=True)
        p = jnp.exp(s - m)
        o = _dot(p.astype(BF16), v_ref[h]) / jnp.sum(p, axis=-1, keepdims=True)
        outs.append(o)
    o_ref[...] = jnp.concatenate(outs, axis=-1).astype(o_ref.dtype)


def mem_attention(qsrc, q_col, kt, v, *, heads, width, tq, out_dtype):
    bsz, t_len, _ = qsrc.shape
    dh = width // heads
    m_len = v.shape[2]
    return pl.pallas_call(
        functools.partial(_mem_attn_kernel, heads=heads, scale=dh ** -0.5),
        grid=(bsz, t_len // tq),
        in_specs=[pl.BlockSpec((None, tq, width), lambda b, i: (b, i, q_col)),
                  pl.BlockSpec((None, heads, dh, m_len), lambda b, i: (b, 0, 0, 0)),
                  pl.BlockSpec((None, heads, m_len, dh), lambda b, i: (b, 0, 0, 0))],
        out_specs=pl.BlockSpec((None, tq, width), lambda b, i: (b, i, 0)),
        out_shape=jax.ShapeDtypeStruct((bsz, t_len, width), out_dtype),
        compiler_params=_cparams(("parallel", "parallel")),
        name="mem_attention",
    )(qsrc, kt, v)


def _pack_bf16_pair(lo, hi):
    lo_bits = lax.bitcast_convert_type(lo.astype(BF16).astype(F32), jnp.uint32)
    hi_bits = lax.bitcast_convert_type(hi.astype(BF16).astype(F32), jnp.uint32)
    return lax.shift_right_logical(lo_bits, jnp.uint32(16)) | (hi_bits & jnp.uint32(0xFFFF0000))


def _unpack_bf16_pair(u):
    lo = lax.bitcast_convert_type(lax.shift_left(u, jnp.uint32(16)), F32)
    hi = lax.bitcast_convert_type(u & jnp.uint32(0xFFFF0000), F32)
    return lo, hi


def _store_packed_rows(ref, x, nrows):
    half = x.shape[1] // 2
    nslab = half // LANES
    for s in range(nslab):
        lo = x[:, s * LANES:(s + 1) * LANES]
        hi = x[:, half + s * LANES:half + (s + 1) * LANES]
        ref[pl.ds(s, nrows, stride=nslab), :] = _pack_bf16_pair(lo, hi)


def _mix_out_kernel(yr_ref, yf_ref, ym_ref, x_ref, eg_ref, eb_ref, woa_ref, wob_ref, woc_ref,
                    g_ref, b_ref, rw_ref, rb_ref, h_ref, hp_ref, ti_ref, tg_ref, *, alpha, n_experts):
    bm, d = x_ref.shape
    h0 = _layer_norm_rows(x_ref[...], eg_ref[...], eb_ref[...], LN_EPS)
    mixed = _dot(yr_ref[...], woa_ref[...]) + _dot(yf_ref[...], wob_ref[...]) + _dot(ym_ref[...], woc_ref[...])
    h1 = _layer_norm_rows(alpha * h0 + mixed, g_ref[...], b_ref[...], LN_EPS)
    h_ref[...] = h1
    _store_packed_rows(hp_ref, h1, bm)

    logits = _dot_bf16x3(h1, rw_ref[...]) + rb_ref[...]
    lane = lax.broadcasted_iota(jnp.int32, logits.shape, 1)
    work = jnp.where(lane < n_experts, logits, -jnp.inf)
    vals, idxs = [], []
    for _ in range(TOP_K):
        m = jnp.max(work, axis=-1, keepdims=True)
        idx = jnp.min(jnp.where(work == m, lane, LANES), axis=-1, keepdims=True)
        work = jnp.where(lane == idx, -jnp.inf, work)
        vals.append(m)
        idxs.append(idx)
    exps = [jnp.exp(v_ - vals[0]) for v_ in vals]
    denom = exps[0] + exps[1] + exps[2] + exps[3]
    ti = jnp.zeros(logits.shape, jnp.int32)
    tg = jnp.zeros(logits.shape, F32)
    for k_ in range(TOP_K):
        ti = jnp.where(lane == k_, idxs[k_], ti)
        tg = jnp.where(lane == k_, exps[k_] / denom, tg)
    ti_ref[...] = ti
    tg_ref[...] = tg


def mix_out(yr, yf, ym, x, emb_g, emb_b, wo, ln_g, ln_b, router_w, router_b, *, alpha, n_experts, bm):
    n, d = x.shape
    wr, wf, wm = yr.shape[1], yf.shape[1], ym.shape[1]
    assert wf == wm and wr % wf == 0 and d % (2 * LANES) == 0
    nslab = d // (2 * LANES)
    row = lambda w_: pl.BlockSpec((bm, w_), lambda i: (i, 0))
    const = lambda r_, w_, ri=0: pl.BlockSpec((r_, w_), lambda i: (ri, 0))
    return pl.pallas_call(
        functools.partial(_mix_out_kernel, alpha=alpha, n_experts=n_experts),
        grid=(n // bm,),
        in_specs=[row(wr), row(wf), row(wm), row(d), const(1, d), const(1, d),
                  const(wr, d), const(wf, d, wr // wf), const(wm, d, wr // wf + 1),
                  const(1, d), const(1, d), const(d, LANES), const(1, LANES)],
        out_specs=[row(d), pl.BlockSpec((bm * nslab, LANES), lambda i: (i, 0)), row(LANES), row(LANES)],
        out_shape=[jax.ShapeDtypeStruct((n, d), F32),
                   jax.ShapeDtypeStruct((n * nslab, LANES), jnp.uint32),
                   jax.ShapeDtypeStruct((n, LANES), jnp.int32),
                   jax.ShapeDtypeStruct((n, LANES), F32)],
        compiler_params=_cparams(("parallel",)),
        name="mix_out",
    )(yr, yf, ym, x, emb_g.reshape(1, d), emb_b.reshape(1, d), wo, wo, wo,
      ln_g.reshape(1, d), ln_b.reshape(1, d), router_w, router_b)


def _moe_kernel(be_ref, nu_ref, tok_ref, h_ref, w1g_ref, w1l_ref, b1g_ref, b1l_ref, w2_ref, b2_ref,
                o_ref, idx_ref, xraw_ref, xb_ref, acc_ref, idx_sem, row_sem, *, nslab):
    bm = xb_ref.shape[0]
    i = pl.program_id(0)
    f = pl.program_id(1)
    nf = pl.num_programs(1)
    n_used = nu_ref[0]

    rows_per_step = bm // nf

    def fetch_idx(blk):
        cp = pltpu.make_async_copy(tok_ref.at[blk], idx_ref, idx_sem)
        cp.start()
        cp.wait()

    def issue_row(r):
        src = pl.multiple_of(idx_ref[r] * nslab, nslab)
        dst = pl.multiple_of(r * nslab, nslab)
        pltpu.make_async_copy(h_ref.at[pl.ds(src, nslab), :], xraw_ref.at[pl.ds(dst, nslab), :], row_sem).start()

    def wait_rows():
        pltpu.make_async_copy(h_ref.at[pl.ds(0, bm * nslab), :], xraw_ref, row_sem).wait()

    @pl.when(i < n_used)
    def _():
        @pl.when(f == 0)
        def _():
            @pl.when(i == 0)
            def _():
                acc_ref[...] = jnp.zeros_like(acc_ref)
                fetch_idx(0)

                def issue(r, carry):
                    issue_row(r)
                    return carry

                lax.fori_loop(0, bm, issue, 0, unroll=8)

            wait_rows()
            half = nslab * LANES
            for s in range(nslab):
                lo, hi = _unpack_bf16_pair(xraw_ref[pl.ds(s, bm, stride=nslab), :])
                xb_ref[:, s * LANES:(s + 1) * LANES] = lo.astype(BF16)
                xb_ref[:, half + s * LANES:half + (s + 1) * LANES] = hi.astype(BF16)
            fetch_idx(jnp.minimum(i + 1, n_used - 1))

        for r in range(rows_per_step):
            issue_row(f * rows_per_step + r)

        xb = xb_ref[...]
        hg = _dot(xb, w1g_ref[...].astype(BF16)) + b1g_ref[...]
        hl = _dot(xb, w1l_ref[...].astype(BF16)) + b1l_ref[...]
        glu = jnp.minimum(hg, SWIGLU_LIMIT)
        lin = jnp.clip(hl, -SWIGLU_LIMIT, SWIGLU_LIMIT)
        act = (lin + 1.0) * glu * _sigmoid(SWIGLU_ALPHA * glu)
        acc_ref[...] = jnp.where(f == 0, 0.0, acc_ref[...]) + _dot(act.astype(BF16), w2_ref[...].astype(BF16))

        @pl.when(f == nf - 1)
        def _():
            _store_packed_rows(o_ref, acc_ref[...] + b2_ref[...], bm)

            @pl.when(i == n_used - 1)
            def _():
                wait_rows()

    @pl.when((i >= n_used) & (f == nf - 1))
    def _():
        o_ref[...] = jnp.zeros_like(o_ref)


def moe_experts(block_e, n_used, tok_blocks, h_rows, w1, b1, w2, b2, *, bm, tf):
    nblk = tok_blocks.shape[0]
    n_exp, d, de2 = w1.shape
    de = de2 // 2
    nslab = d // (2 * LANES)
    nf = de // tf
    assert bm % nf == 0

    def blk(i, nu):
        return jnp.minimum(i, nu[0] - 1)

    def fi(i, f, nu):
        return jnp.where(i < nu[0], f, nf - 1)

    grid_spec = pltpu.PrefetchScalarGridSpec(
        num_scalar_prefetch=2,
        grid=(nblk, nf),
        in_specs=[
            pl.BlockSpec(memory_space=pl.ANY),
            pl.BlockSpec(memory_space=pl.ANY),
            pl.BlockSpec((None, d, tf), lambda i, f, be, nu: (be[blk(i, nu)], 0, fi(i, f, nu))),
            pl.BlockSpec((None, d, tf), lambda i, f, be, nu: (be[blk(i, nu)], 0, nf + fi(i, f, nu))),
            pl.BlockSpec((None, 1, tf), lambda i, f, be, nu: (be[blk(i, nu)], 0, fi(i, f, nu))),
            pl.BlockSpec((None, 1, tf), lambda i, f, be, nu: (be[blk(i, nu)], 0, nf + fi(i, f, nu))),
            pl.BlockSpec((None, tf, d), lambda i, f, be, nu: (be[blk(i, nu)], fi(i, f, nu), 0)),
            pl.BlockSpec((None, 1, d), lambda i, f, be, nu: (be[blk(i, nu)], 0, 0)),
        ],
        out_specs=pl.BlockSpec((bm * nslab, LANES), lambda i, f, be, nu: (i, 0)),
        scratch_shapes=[
            pltpu.SMEM((bm,), jnp.int32),
            pltpu.VMEM((bm * nslab, LANES), jnp.uint32),
            pltpu.VMEM((bm, d), BF16),
            pltpu.VMEM((bm, d), F32),
            pltpu.SemaphoreType.DMA,
            pltpu.SemaphoreType.DMA,
        ],
    )
    return pl.pallas_call(
        functools.partial(_moe_kernel, nslab=nslab),
        grid_spec=grid_spec,
        out_shape=jax.ShapeDtypeStruct((nblk * bm * nslab, LANES), jnp.uint32),
        compiler_params=_cparams(("arbitrary", "arbitrary")),
        name="moe_experts",
    )(block_e, n_used, tok_blocks, h_rows, w1, w1, b1, b1, w2, b2)


def _combine_kernel(pos_ref, y_ref, h_ref, tg_ref, g_ref, b_ref, o_ref, idx_ref, ybuf_ref, idx_sem, row_sem,
                    *, alpha, nslab):
    tm = o_ref.shape[0]
    nrow = TOP_K * tm
    i = pl.program_id(0)
    n_steps = pl.num_programs(0)

    def fetch_rows(step, slot):
        cp = pltpu.make_async_copy(pos_ref.at[step], idx_ref.at[slot], idx_sem)
        cp.start()
        cp.wait()

        def issue(r, carry):
            src = pl.multiple_of(idx_ref[slot, r] * nslab, nslab)
            dst = pl.multiple_of(r * nslab, nslab)
            pltpu.make_async_copy(y_ref.at[pl.ds(src, nslab), :], ybuf_ref.at[slot, pl.ds(dst, nslab), :],
                                  row_sem.at[slot]).start()
            return carry

        lax.fori_loop(0, nrow, issue, 0, unroll=8)

    slot = i % 2

    @pl.when(i == 0)
    def _():
        fetch_rows(0, 0)

    @pl.when(i + 1 < n_steps)
    def _():
        fetch_rows(i + 1, 1 - slot)

    pltpu.make_async_copy(y_ref.at[pl.ds(0, nrow * nslab), :], ybuf_ref.at[slot], row_sem.at[slot]).wait()

    gates = [tg_ref[:, k_:k_ + 1] for k_ in range(TOP_K)]
    lo_slabs, hi_slabs = [], []
    for s in range(nslab):
        acc_lo = acc_hi = None
        for k_ in range(TOP_K):
            lo, hi = _unpack_bf16_pair(ybuf_ref[slot, pl.ds(k_ * tm * nslab + s, tm, stride=nslab), :])
            acc_lo = gates[k_] * lo if acc_lo is None else acc_lo + gates[k_] * lo
            acc_hi = gates[k_] * hi if acc_hi is None else acc_hi + gates[k_] * hi
        lo_slabs.append(acc_lo)
        hi_slabs.append(acc_hi)
    pre = alpha * h_ref[...] + jnp.concatenate(lo_slabs + hi_slabs, axis=-1)
    o_ref[...] = _layer_norm_rows(pre, g_ref[...], b_ref[...], LN_EPS).astype(o_ref.dtype)


def moe_combine(pos_blocks, y_rows, h1, gates, ln_g, ln_b, *, alpha, tm, out_dtype):
    n = gates.shape[0]
    d = ln_g.shape[-1]
    nslab = d // (2 * LANES)
    nrow = TOP_K * tm
    return pl.pallas_call(
        functools.partial(_combine_kernel, alpha=alpha, nslab=nslab),
        grid=(n // tm,),
        in_specs=[pl.BlockSpec(memory_space=pl.ANY),
                  pl.BlockSpec(memory_space=pl.ANY),
                  pl.BlockSpec((tm, d), lambda i: (i, 0)),
                  pl.BlockSpec((tm, LANES), lambda i: (i, 0)),
                  pl.BlockSpec((1, d), lambda i: (0, 0)),
                  pl.BlockSpec((1, d), lambda i: (0, 0))],
        out_specs=pl.BlockSpec((tm, d), lambda i: (i, 0)),
        out_shape=jax.ShapeDtypeStruct((n, d), out_dtype),
        scratch_shapes=[pltpu.SMEM((2, nrow), jnp.int32),
                        pltpu.VMEM((2, nrow * nslab, LANES), jnp.uint32),
                        pltpu.SemaphoreType.DMA,
                        pltpu.SemaphoreType.DMA((2,))],
        compiler_params=_cparams(("arbitrary",)),
        name="moe_combine",
    )(pos_blocks, y_rows, h1, gates, ln_g.reshape(1, d), ln_b.reshape(1, d))


def _route(top_idx, n_experts, bm, tm):
    n = top_idx.shape[0]
    n_assign = n * TOP_K
    e = top_idx.reshape(-1)
    onehot = (e[:, None] == jnp.arange(n_experts, dtype=jnp.int32)[None, :]).astype(jnp.int32)
    csum = jnp.cumsum(onehot, axis=0)
    rank = jnp.sum((csum - onehot) * onehot, axis=-1)
    counts = csum[-1]
    padded = ((counts + bm - 1) // bm) * bm
    pend = jnp.cumsum(padded)
    pstart = pend - padded
    pos = jnp.sum(onehot * pstart[None, :], axis=-1) + rank
    nblk = n_assign // bm + n_experts
    tok = jnp.arange(n_assign, dtype=jnp.int32) // TOP_K
    buf_tok = jnp.zeros((nblk * bm,), jnp.int32).at[pos].set(tok)
    block_e = jnp.clip(jnp.searchsorted(pend, jnp.arange(nblk, dtype=jnp.int32) * bm, side="right"),
                       0, n_experts - 1).astype(jnp.int32)
    n_used = (pend[-1] // bm).astype(jnp.int32).reshape(1)
    pos_blocks = pos.reshape(n // tm, tm, TOP_K).transpose(0, 2, 1).reshape(n // tm, TOP_K * tm)
    return block_e, n_used, buf_tok.reshape(nblk, bm), pos_blocks.astype(jnp.int32)


RWKV_HEADS = 16
DECAY_LORA = 96
ICLR_LORA = 96
GATE_LORA = 256
FOX_HEADS = 8
MEM_HEADS = 4
MEM_HEAD_DIM = 128
N_EXPERTS = 32
LORA_PAD = 128

TILES = dict(proj_bm=1024, proj_bn=512, rwkv_tt=1024, rwkv_gw=256, fox_tt=512, fox_bq=2048, fox_sub=256, mem_tq=1024,
             mix_bm=512, moe_bm=1024, moe_tf=256, comb_tm=512)


def _bf16_floor(a):
    bits = lax.bitcast_convert_type(a, jnp.uint32) & jnp.uint32(0xFFFF0000)
    return lax.bitcast_convert_type(bits, F32)


def _pad_cols(a, width):
    return jnp.pad(a, [(0, 0)] * (a.ndim - 1) + [(0, width - a.shape[-1])])


def _layer(x, mem_n_kv, emb_ln_g, emb_ln_b, w_in, mu, w0, wd, a0, wa, wg, k_k, k_a, r_k, gn_g, gn_b, b_f,
           w_o, ln1_g, ln1_b, router_w, router_b, w1, b1, w2, b2, ln2_g, ln2_b, alpha, tiles):
    bsz, t_len, d = x.shape
    n = bsz * t_len
    wr = RWKV_HEADS * HEAD64
    wf = FOX_HEADS * HEAD64
    wm = MEM_HEADS * MEM_HEAD_DIM
    c0 = 3 * wr
    c1 = c0 + DECAY_LORA
    c2 = c1 + ICLR_LORA
    c3 = c2 + GATE_LORA
    f0 = c3 + 3 * wf
    f1 = f0 + FOX_HEADS
    tl = lambda name, dim: min(tiles[name], dim)

    def rwkv_cols(a):
        return jnp.concatenate([a[..., :c0], _pad_cols(a[..., c0:c1], LORA_PAD),
                                _pad_cols(a[..., c1:c2], LORA_PAD), a[..., c2:c3]], axis=-1)

    w_rwkv = rwkv_cols(w_in).astype(BF16)
    w_qkvm = jnp.concatenate([w_in[:, c3:f0], w_in[:, f1:]], axis=-1).astype(BF16)
    w_f = _pad_cols(w_in[:, f0:f1], LANES).astype(BF16)
    x2 = x.reshape(n, d)
    bn = tl("proj_bn", w_qkvm.shape[1])
    p_rwkv, p_qkvm, f_log = ln_matmul(x2, emb_ln_g, emb_ln_b, [w_rwkv, w_qkvm, _pad_cols(w_f, bn)],
                                      [F32, BF16, F32], tl("proj_bm", n), bn)

    zrows = lambda a, rows: jnp.pad(a, ((0, rows - a.shape[0]), (0, 0)))
    y_rwkv = rwkv_mix(p_rwkv.reshape(bsz, t_len, -1), rwkv_cols(mu), w0, a0, k_k, k_a, r_k.reshape(-1), gn_g, gn_b,
                      zrows(wd, LORA_PAD), zrows(wa, LORA_PAD), wg,
                      width=wr, tt=tl("rwkv_tt", t_len), gw=tl("rwkv_gw", wr), out_dtype=BF16)

    c = fox_cumsum(f_log.reshape(bsz, t_len, -1), _pad_cols(b_f.reshape(1, -1), LANES), tl("fox_tt", t_len))
    bh = bsz * FOX_HEADS
    bq = tl("fox_bq", t_len)
    c_rows = c[..., :FOX_HEADS].transpose(0, 2, 1).reshape(bh, 1, t_len)
    bias = (c_rows.reshape(bh, 1, t_len // bq, bq)[..., :1] - c_rows.reshape(bh, 1, t_len // bq, bq)).reshape(bh, 1, t_len)
    b_hi = _bf16_floor(bias)
    b_mid = _bf16_floor(bias - b_hi)
    b_lo = bias - b_hi - b_mid
    b_hi, b_mid, b_lo = b_hi.astype(BF16), b_mid.astype(BF16), b_lo.astype(BF16)
    qkv = p_qkvm[:, :3 * wf].reshape(bsz, t_len, 3, FOX_HEADS, HEAD64)
    qkv = qkv.transpose(2, 0, 3, 1, 4).reshape(3, bh, t_len, HEAD64)
    q_scaled = (qkv[0].astype(F32) * (HEAD64 ** -0.5 * LOG2E)).astype(BF16)
    q_aug = jnp.concatenate([q_scaled, jnp.ones((bh, t_len, 3), BF16),
                             jnp.zeros((bh, t_len, LANES - HEAD64 - 3), BF16)], axis=-1)
    kt_aug = jnp.concatenate([qkv[1].transpose(0, 2, 1), b_hi, b_mid, b_lo,
                              jnp.zeros((bh, LANES - HEAD64 - 3, t_len), BF16)], axis=1)
    v_aug = jnp.concatenate([qkv[2], jnp.ones((bh, t_len, 1), BF16),
                             jnp.zeros((bh, t_len, LANES - HEAD64 - 1), BF16)], axis=-1)
    o_fox = fox_attention(q_aug, kt_aug, v_aug, c_rows, bq=bq, sub=tl("fox_sub", bq), dh=HEAD64, out_dtype=BF16)
    y_fox = o_fox.reshape(bsz, FOX_HEADS, t_len, HEAD64).transpose(0, 2, 1, 3).reshape(n, wf)

    m_len = mem_n_kv.shape[0] // bsz
    km = mem_n_kv[:, :wm].reshape(bsz, m_len, MEM_HEADS, MEM_HEAD_DIM).transpose(0, 2, 3, 1).astype(BF16)
    vm = mem_n_kv[:, wm:].reshape(bsz, m_len, MEM_HEADS, MEM_HEAD_DIM).transpose(0, 2, 1, 3).astype(BF16)
    y_mem = mem_attention(p_qkvm.reshape(bsz, t_len, -1), (3 * wf) // wm, km, vm, heads=MEM_HEADS, width=wm,
                          tq=tl("mem_tq", t_len), out_dtype=BF16)

    rw = _pad_cols(router_w, LANES)
    rb = _pad_cols(router_b.reshape(1, -1), LANES)
    h1, h_rows, top_i, top_g = mix_out(y_rwkv.reshape(n, wr), y_fox, y_mem.reshape(n, wm), x2, emb_ln_g, emb_ln_b,
                                   w_o.astype(BF16), ln1_g, ln1_b, rw, rb, alpha=alpha, n_experts=N_EXPERTS,
                                   bm=tl("mix_bm", n))

    moe_bm = tl("moe_bm", n)
    tm = tl("comb_tm", n)
    block_e, n_used, tok_blocks, pos_blocks = _route(top_i[:, :TOP_K], N_EXPERTS, moe_bm, tm)
    y_rows = moe_experts(block_e, n_used, tok_blocks, h_rows, w1, b1[:, None, :], w2, b2[:, None, :],
                         bm=moe_bm, tf=tl("moe_tf", w2.shape[1]))
    out = moe_combine(pos_blocks, y_rows, h1, top_g, ln2_g, ln2_b, alpha=alpha, tm=tm, out_dtype=x.dtype)
    return out.reshape(bsz, t_len, d)


def kernel(x, mem, emb_ln_g, emb_ln_b, w_in, rwkv_mu, rwkv_w0, rwkv_w_decay_up, rwkv_a0, rwkv_w_iclr_up,
           rwkv_w_gate_up, rwkv_k_k, rwkv_k_a, rwkv_r_k, rwkv_gn_g, rwkv_gn_b, fox_b_f, mem_ln_g, mem_ln_b,
           w_mem_kv, w_o, ln1_g, ln1_b, router_w, router_b, expert_w1, expert_b1, expert_w2, expert_b2,
           ln2_g, ln2_b):
    depth = w_in.shape[0]
    assert depth == 1
    alpha = (2 * depth) ** 0.25
    bsz, m_len, d = mem.shape
    kv, = ln_matmul(mem.reshape(bsz * m_len, d), mem_ln_g, mem_ln_b, [w_mem_kv[0].astype(BF16)], [F32],
                    min(512, bsz * m_len), w_mem_kv.shape[-1])
    return _layer(x, kv, emb_ln_g, emb_ln_b, w_in[0], rwkv_mu[0], rwkv_w0[0], rwkv_w_decay_up[0], rwkv_a0[0],
                  rwkv_w_iclr_up[0], rwkv_w_gate_up[0], rwkv_k_k[0], rwkv_k_a[0], rwkv_r_k[0], rwkv_gn_g[0],
                  rwkv_gn_b[0], fox_b_f[0], w_o[0], ln1_g[0], ln1_b[0], router_w[0], router_b[0],
                  expert_w1[0], expert_b1[0], expert_w2[0], expert_b2[0], ln2_g[0], ln2_b[0], alpha, TILES)
```
